```python
import math
import jax, jax.numpy as jnp
from jax import lax
import numpy as np

D_MODEL = 2048
BATCH = 1
SEQ = 16384
DEPTH = 1
DEC_BATCH = 4
DEC_SEQ = 4096
PAST_LEN = 128

GRID_W = 64
NA_HEADS = 8
NA_HEAD_DIM = 128
NA_WIDTH = NA_HEADS * NA_HEAD_DIM
NA_MAX_KH = 8
NA_KW = 16
NA_QB = 16
NA_KB = NA_QB + NA_KW
NEG_INF = -1e30
LRU_WIDTH = 1024
LRU_BLOCKS = 8
LRU_BLOCK = LRU_WIDTH // LRU_BLOCKS
CONV_W = 4
LRU_C = 8.0
PEER_HEADS = 8
PEER_NKEYS = 128
PEER_N = PEER_NKEYS * PEER_NKEYS
PEER_DK = 256
PEER_TOPK = 16
PEER_CHUNK = 128
PLE_DIM = 256
RMS_EPS = 1e-6

IN_SPLITS = [NA_WIDTH, 2 * NA_WIDTH, 3 * NA_WIDTH, 3 * NA_WIDTH + LRU_WIDTH,
             3 * NA_WIDTH + 2 * LRU_WIDTH, 3 * NA_WIDTH + 2 * LRU_WIDTH + D_MODEL]
IN_WIDTH = 3 * NA_WIDTH + 2 * LRU_WIDTH + 2 * D_MODEL

kernel_name = 'hybrid_na_rglru_peer_encoder'


def rms_norm(x, g):
    xf = x.astype(jnp.float32)
    y = xf * lax.rsqrt(jnp.mean(xf * xf, axis=-1, keepdims=True) + RMS_EPS)
    return (y * g.astype(jnp.float32)).astype(x.dtype)


def neighbourhood_attention(q, k, v, rpb):
    B, S = q.shape[0], q.shape[1]
    rows = S // GRID_W
    kh = min(NA_MAX_KH, rows)
    to_grid = lambda t: t.reshape(B, rows, GRID_W, NA_HEADS, NA_HEAD_DIM)
    qg, kg, vg = to_grid(q), to_grid(k), to_grid(v)
    r = jnp.arange(rows)
    row_start = jnp.clip(r - kh // 2, 0, rows - kh)
    key_rows = row_start[:, None] + jnp.arange(kh)[None, :]
    dr_idx = key_rows - r[:, None] + (NA_MAX_KH - 1)
    scale = NA_HEAD_DIM ** -0.5
    outs = []
    for j in range(GRID_W // NA_QB):
        q0 = j * NA_QB
        k0 = min(max(q0 - NA_KW // 2, 0), GRID_W - NA_KB)
        qc = np.arange(q0, q0 + NA_QB)
        kc = np.arange(k0, k0 + NA_KB)
        col_start = np.clip(qc - NA_KW // 2, 0, GRID_W - NA_KW)
        col_ok = (kc[None, :] >= col_start[:, None]) & (kc[None, :] < col_start[:, None] + NA_KW)
        dc_idx = np.clip(kc[None, :] - qc[:, None], -(NA_KW - 1), NA_KW - 1) + (NA_KW - 1)
        qb = qg[:, :, q0:q0 + NA_QB]
        kb = kg[:, :, k0:k0 + NA_KB][:, key_rows]
        vb = vg[:, :, k0:k0 + NA_KB][:, key_rows]
        s = jnp.einsum('brqhd,brikhd->bhrqik', qb, kb, preferred_element_type=jnp.float32) * scale
        bias = rpb[:, dr_idx[:, None, :, None], jnp.asarray(dc_idx)[None, :, None, :]]
        s = s + bias.astype(jnp.float32)[None]
        s = jnp.where(jnp.asarray(col_ok)[None, None, None, :, None, :], s, NEG_INF)
        probs = jax.nn.softmax(s.reshape(B, NA_HEADS, rows, NA_QB, kh * NA_KB), axis=-1).reshape(s.shape)
        outs.append(jnp.einsum('bhrqik,brikhd->brqhd', probs.astype(v.dtype), vb))
    out = jnp.concatenate(outs, axis=2)
    return out.reshape(B, S, NA_WIDTH)


def depthwise_conv_centred(x, w, b):
    C = x.shape[-1]
    y = lax.conv_general_dilated(
        x, w[:, None, :].astype(x.dtype), window_strides=(1,),
        padding=[(CONV_W // 2, CONV_W - 1 - CONV_W // 2)],
        dimension_numbers=('NWC', 'WIO', 'NWC'), feature_group_count=C)
    return y + b.astype(x.dtype)


def rg_lru_scan(xf, wa, ba, wx, bx, lam, reverse):
    B, S, C = xf.shape
    xb = xf.reshape(B, S, LRU_BLOCKS, LRU_BLOCK)
    rec = jax.nn.sigmoid(jnp.einsum('bsnc,ncd->bsnd', xb, wa.astype(jnp.float32)).reshape(B, S, C) + ba.astype(jnp.float32))
    inp = jax.nn.sigmoid(jnp.einsum('bsnc,ncd->bsnd', xb, wx.astype(jnp.float32)).reshape(B, S, C) + bx.astype(jnp.float32))
    log_a = -LRU_C * rec * jax.nn.softplus(-lam.astype(jnp.float32))
    a = jnp.exp(log_a)
    u = jnp.sqrt(-jnp.expm1(2.0 * log_a)) * (inp * xf)

    def combine(e1, e2):
        a1, b1 = e1
        a2, b2 = e2
        return a1 * a2, a2 * b1 + b2

    _, h = lax.associative_scan(combine, (a, u), axis=1, reverse=reverse)
    return h


def bidir_rg_lru(x, wa, ba, wx, bx, lam):
    xf = x.astype(jnp.float32)
    fwd = rg_lru_scan(xf, wa[0], ba[0], wx[0], bx[0], lam[0], False)
    bwd = rg_lru_scan(xf, wa[1], ba[1], wx[1], bx[1], lam[1], True)
    return (fwd + bwd).astype(x.dtype)


def peer(x, wq, subkeys, u, v):
    B, S, D = x.shape
    T = B * S
    xt = x.reshape(T, D)
    q = (xt @ wq).reshape(T, PEER_HEADS, 2, PEER_DK // 2)
    s = jnp.einsum('thpd,hpkd->thpk', q, subkeys, preferred_element_type=jnp.float32)
    sv, si = lax.top_k(s, PEER_TOPK)
    cand = sv[:, :, 0, :, None] + sv[:, :, 1, None, :]
    cv, ci = lax.top_k(cand.reshape(T, PEER_HEADS, PEER_TOPK * PEER_TOPK), PEER_TOPK)
    i1 = jnp.take_along_axis(si[:, :, 0], ci // PEER_TOPK, axis=-1)
    i2 = jnp.take_along_axis(si[:, :, 1], ci % PEER_TOPK, axis=-1)
    ids = i1 * PEER_NKEYS + i2
    g = jax.nn.softmax(cv, axis=-1)
    n_sel = PEER_HEADS * PEER_TOPK
    n_chunk = T // PEER_CHUNK

    def chunk(args):
        xc, idc, gc = args
        hc = jax.nn.gelu(jnp.einsum('cnd,cd->cn', u[idc], xc, preferred_element_type=jnp.float32), approximate=False)
        return jnp.einsum('cn,cnd->cd', (gc * hc).astype(x.dtype), v[idc]).astype(x.dtype)

    y = lax.map(chunk, (xt.reshape(n_chunk, PEER_CHUNK, D),
                        ids.reshape(n_chunk, PEER_CHUNK, n_sel),
                        g.reshape(n_chunk, PEER_CHUNK, n_sel)))
    return y.reshape(B, S, D)


def trunk(x, p, w):
    B, S = x.shape[0], x.shape[1]
    h = x
    for l in range(DEPTH):
        xn = rms_norm(h, w['norm_mix'][l])
        z = xn @ w['w_in'][l]
        q, k, vv, xr, gr, ga, gb = jnp.split(z, IN_SPLITS, axis=-1)
        heads = lambda t: t.reshape(B, S, NA_HEADS, NA_HEAD_DIM)
        y_a = neighbourhood_attention(heads(q), heads(k), heads(vv), w['na_rpb'][l])
        xc = depthwise_conv_centred(xr, w['conv_w'][l], w['conv_b'][l])
        y_r = bidir_rg_lru(xc, w['lru_wa'][l], w['lru_ba'][l], w['lru_wx'][l], w['lru_bx'][l], w['lru_lambda'][l]) * jax.nn.gelu(gr)
        merged = jax.nn.sigmoid(ga) * (y_a @ w['w_branch_a'][l]) + jax.nn.sigmoid(gb) * (y_r @ w['w_branch_r'][l])
        h = h + (merged @ w['w_out'][l]).astype(h.dtype)
        h = h + peer(rms_norm(h, w['norm_ffn'][l]), w['peer_wq'][l], w['peer_subkeys'][l], w['peer_u'][l], w['peer_v'][l]).astype(h.dtype)
        gate = jax.nn.sigmoid(rms_norm(h, w['norm_ple'][l]) @ w['ple_gate_w'][l])
        h = h + (gate * (p[l] @ w['ple_proj_w'][l])).astype(h.dtype)
    return rms_norm(h, w['final_norm'])


def setup_inputs(seed: int = 0) -> dict:
    key = jax.random.key(seed)
    ks = jax.random.split(key, 32)
    f32 = jnp.float32
    nrm = lambda kk, shape, sc: jax.random.normal(kk, shape, f32) * sc
    gain = lambda kk, shape: 1.0 + 0.05 * jax.random.normal(kk, shape, f32)
    a_pow = jax.random.uniform(ks[10], (DEPTH, 2, LRU_WIDTH), f32, 0.9, 0.999)
    a_base = a_pow ** (1.0 / LRU_C)
    lru_lambda = jnp.log(a_base) - jnp.log1p(-a_base)
    return {
        'x_prompt': nrm(ks[0], (BATCH, SEQ, D_MODEL), 1.0),
        'x_sample': nrm(ks[1], (DEC_BATCH, DEC_SEQ, D_MODEL), 1.0),
        'p_prompt': nrm(ks[2], (DEPTH, BATCH, SEQ, PLE_DIM), 1.0),
        'p_sample': nrm(ks[3], (DEPTH, DEC_BATCH, DEC_SEQ, PLE_DIM), 1.0),
        'norm_mix': gain(ks[4], (DEPTH, D_MODEL)),
        'w_in': nrm(ks[5], (DEPTH, D_MODEL, IN_WIDTH), D_MODEL ** -0.5),
        'na_rpb': nrm(ks[6], (DEPTH, NA_HEADS, 2 * NA_MAX_KH - 1, 2 * NA_KW - 1), 0.1),
        'conv_w': nrm(ks[7], (DEPTH, CONV_W, LRU_WIDTH), CONV_W ** -0.5),
        'conv_b': nrm(ks[8], (DEPTH, LRU_WIDTH), 0.01),
        'lru_wa': nrm(ks[9], (DEPTH, 2, LRU_BLOCKS, LRU_BLOCK, LRU_BLOCK), LRU_BLOCK ** -0.5),
        'lru_ba': nrm(ks[11], (DEPTH, 2, LRU_WIDTH), 0.01),
        'lru_wx': nrm(ks[12], (DEPTH, 2, LRU_BLOCKS, LRU_BLOCK, LRU_BLOCK), LRU_BLOCK ** -0.5),
        'lru_bx': nrm(ks[13], (DEPTH, 2, LRU_WIDTH), 0.01),
        'lru_lambda': lru_lambda,
        'w_branch_a': nrm(ks[14], (DEPTH, NA_WIDTH, D_MODEL), NA_WIDTH ** -0.5),
        'w_branch_r': nrm(ks[15], (DEPTH, LRU_WIDTH, D_MODEL), LRU_WIDTH ** -0.5),
        'w_out': nrm(ks[16], (DEPTH, D_MODEL, D_MODEL), D_MODEL ** -0.5),
        'norm_ffn': gain(ks[17], (DEPTH, D_MODEL)),
        'peer_wq': nrm(ks[18], (DEPTH, D_MODEL, PEER_HEADS * PEER_DK), D_MODEL ** -0.5),
        'peer_subkeys': nrm(ks[19], (DEPTH, PEER_HEADS, 2, PEER_NKEYS, PEER_DK // 2), (PEER_DK // 2) ** -0.5),
        'peer_u': nrm(ks[20], (DEPTH, PEER_N, D_MODEL), D_MODEL ** -0.5),
        'peer_v': nrm(ks[21], (DEPTH, PEER_N, D_MODEL), (PEER_HEADS * PEER_TOPK) ** -0.5),
        'norm_ple': gain(ks[22], (DEPTH, D_MODEL)),
        'ple_gate_w': nrm(ks[23], (DEPTH, D_MODEL, D_MODEL), D_MODEL ** -0.5),
        'ple_proj_w': nrm(ks[24], (DEPTH, PLE_DIM, D_MODEL), PLE_DIM ** -0.5),
        'final_norm': gain(ks[25], (D_MODEL,)),
    }


def reference(x_prompt, x_sample, p_prompt, p_sample, norm_mix, w_in, na_rpb, conv_w, conv_b,
              lru_wa, lru_ba, lru_wx, lru_bx, lru_lambda, w_branch_a, w_branch_r, w_out,
              norm_ffn, peer_wq, peer_subkeys, peer_u, peer_v, norm_ple, ple_gate_w, ple_proj_w,
              final_norm):
    w = dict(norm_mix=norm_mix, w_in=w_in, na_rpb=na_rpb, conv_w=conv_w, conv_b=conv_b,
             lru_wa=lru_wa, lru_ba=lru_ba, lru_wx=lru_wx, lru_bx=lru_bx, lru_lambda=lru_lambda,
             w_branch_a=w_branch_a, w_branch_r=w_branch_r, w_out=w_out, norm_ffn=norm_ffn,
             peer_wq=peer_wq, peer_subkeys=peer_subkeys, peer_u=peer_u, peer_v=peer_v,
             norm_ple=norm_ple, ple_gate_w=ple_gate_w, ple_proj_w=ple_proj_w, final_norm=final_norm)
    y_prompt = trunk(x_prompt, p_prompt, w)
    y_sample = trunk(x_sample, p_sample, w)
    return (y_prompt, y_sample)
```

```python
import functools
import math

import numpy as np
import jax
import jax.numpy as jnp
from jax import lax
from jax.experimental import pallas as pl
from jax.experimental.pallas import tpu as pltpu

F32 = jnp.float32
BF16 = jnp.bfloat16
I32 = jnp.int32

LANES = 128
GRID_W = 64
NA_HEAD_DIM = 128
NA_KH = 8
NA_KW = 16
NA_ROWS_PER_STEP = 8
NEG_INF = -1e30
LRU_BLOCK = 128
LRU_C = 8.0
LRU_HALO = 8
PEER_NKEYS = 128
PEER_TOPK = 16
PEER_TOPK_LOG2 = 4
assert 1 << PEER_TOPK_LOG2 == PEER_TOPK
RMS_EPS = 1e-6
VMEM_LIMIT = 50 * 1024 * 1024


def _rms(x, g):
    return x * lax.rsqrt(jnp.mean(x * x, axis=-1, keepdims=True) + RMS_EPS) * g


def _gelu(x):
    return 0.5 * x * (1.0 + lax.erf(x * np.float32(math.sqrt(0.5))))


def _sigmoid(x):
    return jax.nn.sigmoid(x)


def _params(*sem):
    return pltpu.CompilerParams(dimension_semantics=sem, vmem_limit_bytes=VMEM_LIMIT)


def _norm_matmul_kernel(x_ref, g_ref, w_ref, o_ref, xn_ref):
    @pl.when(pl.program_id(1) == 0)
    def _():
        xn_ref[...] = _rms(x_ref[...], g_ref[...]).astype(BF16)

    o_ref[...] = jnp.dot(xn_ref[...], w_ref[...], preferred_element_type=F32).astype(o_ref.dtype)


def _norm_matmul(x, g, w, out_dtype, tm, tn):
    T, D = x.shape
    N = w.shape[1]
    tm, tn = min(tm, T), min(tn, N)
    return pl.pallas_call(
        _norm_matmul_kernel,
        grid=(T // tm, N // tn),
        in_specs=[
            pl.BlockSpec((tm, D), lambda i, j: (i, 0)),
            pl.BlockSpec((1, D), lambda i, j: (0, 0)),
            pl.BlockSpec((D, tn), lambda i, j: (0, j)),
        ],
        out_specs=pl.BlockSpec((tm, tn), lambda i, j: (i, j)),
        out_shape=jax.ShapeDtypeStruct((T, N), out_dtype),
        scratch_shapes=[pltpu.VMEM((tm, D), BF16)],
        compiler_params=_params("parallel", "arbitrary"),
        name="norm_matmul",
    )(x, g.reshape(1, D), w)


def _matmul_res_kernel(a_ref, w_ref, r_ref, o_ref):
    o_ref[...] = r_ref[...] + jnp.dot(a_ref[...], w_ref[...], preferred_element_type=F32)


def _matmul_res(a, w, res, tm, tn):
    T, K = a.shape
    N = w.shape[1]
    tm, tn = min(tm, T), min(tn, N)
    return pl.pallas_call(
        _matmul_res_kernel,
        grid=(T // tm, N // tn),
        in_specs=[
            pl.BlockSpec((tm, K), lambda i, j: (i, 0)),
            pl.BlockSpec((K, tn), lambda i, j: (0, j)),
            pl.BlockSpec((tm, tn), lambda i, j: (i, j)),
        ],
        out_specs=pl.BlockSpec((tm, tn), lambda i, j: (i, j)),
        out_shape=jax.ShapeDtypeStruct((T, N), F32),
        compiler_params=_params("parallel", "arbitrary"),
        name="matmul_res",
    )(a, w, res)


def _na_bias_table(rpb):
    qc = np.arange(GRID_W)
    kc = np.arange(GRID_W)
    col_start = np.clip(qc - NA_KW // 2, 0, GRID_W - NA_KW)
    ok = (kc[None, :] >= col_start[:, None]) & (kc[None, :] < col_start[:, None] + NA_KW)
    dc = kc[None, :] - qc[:, None] + (NA_KW - 1)
    onehot = ((dc[None] == np.arange(2 * NA_KW - 1)[:, None, None]) & ok[None]).astype(np.float32)
    t = jnp.einsum("hrd,dqk->hrqk", rpb.astype(F32), onehot, precision=lax.Precision.HIGHEST)
    t = jnp.where(ok[None, None], t, NEG_INF)
    per_off = [jnp.concatenate([t[:, slot - off + NA_KH - 1] for slot in range(NA_KH)], axis=-1)
               for off in range(NA_KH)]
    return jnp.stack(per_off, axis=1)


def _na_kernel(q_ref, kp_ref, kc_ref, kn_ref, vp_ref, vc_ref, vn_ref, b_ref, o_ref, kbuf, vbuf,
               *, rows, heads):
    i = pl.program_id(1)
    tb = NA_ROWS_PER_STEP * GRID_W
    nkeys = NA_KH * GRID_W
    kbuf[0:tb] = kp_ref[...]
    kbuf[tb:2 * tb] = kc_ref[...]
    kbuf[2 * tb:3 * tb] = kn_ref[...]
    vbuf[0:tb] = vp_ref[...]
    vbuf[tb:2 * tb] = vc_ref[...]
    vbuf[2 * tb:3 * tb] = vn_ref[...]
    scale = np.float32(NA_HEAD_DIM ** -0.5)

    def row_body(rr, carry):
        r = i * NA_ROWS_PER_STEP + rr
        rs = jnp.clip(r - NA_KH // 2, 0, rows - NA_KH)
        woff = r - rs
        koff = pl.multiple_of((rs - (i - 1) * NA_ROWS_PER_STEP) * GRID_W, GRID_W)
        qoff = pl.multiple_of(rr * GRID_W, GRID_W)
        for h in range(heads):
            cs = slice(h * NA_HEAD_DIM, (h + 1) * NA_HEAD_DIM)
            q = q_ref[pl.ds(qoff, GRID_W), cs]
            k = kbuf[pl.ds(koff, nkeys), cs]
            v = vbuf[pl.ds(koff, nkeys), cs]
            s = lax.dot_general(q, k, (((1,), (1,)), ((), ())), preferred_element_type=F32)
            s = s * scale + b_ref[h, woff]
            m = jnp.max(s, axis=-1, keepdims=True)
            p = jnp.exp(s - m)
            l = jnp.sum(p, axis=-1, keepdims=True)
            out = jnp.dot(p.astype(BF16), v, preferred_element_type=F32) / l
            o_ref[pl.ds(qoff, GRID_W), cs] = out.astype(o_ref.dtype)
        return carry

    lax.fori_loop(0, NA_ROWS_PER_STEP, row_body, 0)


def _na(qkv, bias):
    B, S, W3 = qkv.shape
    W = W3 // 3
    heads = W // NA_HEAD_DIM
    rows = S // GRID_W
    assert rows % NA_ROWS_PER_STEP == 0 and rows >= NA_KH
    nblk = rows // NA_ROWS_PER_STEP
    tb = NA_ROWS_PER_STEP * GRID_W

    def spec(col, shift):
        return pl.BlockSpec((None, tb, W), lambda b, i: (b, jnp.clip(i + shift, 0, nblk - 1), col))

    return pl.pallas_call(
        functools.partial(_na_kernel, rows=rows, heads=heads),
        grid=(B, nblk),
        in_specs=[spec(0, 0), spec(1, -1), spec(1, 0), spec(1, 1), spec(2, -1), spec(2, 0), spec(2, 1),
                  pl.BlockSpec(bias.shape, lambda b, i: (0, 0, 0, 0))],
        out_specs=pl.BlockSpec((None, tb, W), lambda b, i: (b, i, 0)),
        out_shape=jax.ShapeDtypeStruct((B, S, W), BF16),
        scratch_shapes=[pltpu.VMEM((3 * tb, W), BF16), pltpu.VMEM((3 * tb, W), BF16)],
        compiler_params=_params("parallel", "arbitrary"),
        name="na_attention",
    )(qkv, qkv, qkv, qkv, qkv, qkv, qkv, bias)


def _lru_kernel(xp_ref, x_ref, xn_ref, cw_ref, cb_ref, wa_ref, ba_ref, wx_ref, bx_ref, lam_ref,
                o_ref, xe, a_s, u_s, carry, *, L, nch, nblk):
    d = pl.program_id(0)
    i = pl.program_id(2)
    c = jnp.where(d == 0, i, nch - 1 - i)

    @pl.when(i == 0)
    def _():
        carry[...] = jnp.zeros_like(carry)

    xe[0:LRU_HALO] = jnp.where(c > 0, xp_ref[...], 0.0)
    xe[LRU_HALO:LRU_HALO + L] = x_ref[...]
    xe[LRU_HALO + L:2 * LRU_HALO + L] = jnp.where(c < nch - 1, xn_ref[...], 0.0)
    cw = cw_ref[...]
    xc = cb_ref[...] + cw[2:3] * xe[pl.ds(LRU_HALO, L), :]
    xc = xc + cw[0:1] * xe[pl.ds(LRU_HALO - 2, L), :]
    xc = xc + cw[1:2] * xe[pl.ds(LRU_HALO - 1, L), :]
    xc = xc + cw[3:4] * xe[pl.ds(LRU_HALO + 1, L), :]
    xcb = xc.astype(BF16)

    def blockdiag(w_ref):
        outs = []
        for n in range(nblk):
            cs = slice(n * LRU_BLOCK, (n + 1) * LRU_BLOCK)
            outs.append(jnp.dot(xcb[:, cs], w_ref[n], preferred_element_type=F32))
        return jnp.concatenate(outs, axis=1)

    rec = _sigmoid(blockdiag(wa_ref) + ba_ref[...])
    inp = _sigmoid(blockdiag(wx_ref) + bx_ref[...])
    lam = lam_ref[...]
    softplus_neg_lam = jnp.maximum(-lam, 0.0) + jnp.log1p(jnp.exp(-jnp.abs(lam)))
    log_a = -LRU_C * rec * softplus_neg_lam
    a_s[...] = jnp.exp(log_a)
    y2 = 2.0 * log_a
    e2 = jnp.exp(y2)
    em1 = e2 - 1.0
    expm1 = jnp.where(em1 == 0.0, y2, jnp.where(e2 == 0.0, -1.0, em1 * y2 / jnp.log(e2)))
    u_s[...] = jnp.sqrt(-expm1) * (inp * xc)

    def step(t, h):
        tt = jnp.where(d == 0, t, L - 1 - t)
        h = a_s[pl.ds(tt, 1), :] * h + u_s[pl.ds(tt, 1), :]
        o_ref[pl.ds(tt, 1), :] = h
        return h

    carry[...] = lax.fori_loop(0, L, step, carry[...], unroll=8)


def _lru(rest, conv_w, conv_b, wa, ba, wx, bx, lam, L):
    B, S, _ = rest.shape
    C = conv_w.shape[1]
    nblk = C // LRU_BLOCK
    L = min(L, S)
    nch = S // L
    hb = L // LRU_HALO

    def chunk(d, i):
        return jnp.where(d == 0, i, nch - 1 - i)

    vec = lambda: pl.BlockSpec((None, 1, C), lambda d, b, i: (d, 0, 0))
    mat = lambda: pl.BlockSpec((None, nblk, LRU_BLOCK, LRU_BLOCK), lambda d, b, i: (d, 0, 0, 0))
    return pl.pallas_call(
        functools.partial(_lru_kernel, L=L, nch=nch, nblk=nblk),
        grid=(2, B, nch),
        in_specs=[
            pl.BlockSpec((None, LRU_HALO, C), lambda d, b, i: (b, jnp.maximum(chunk(d, i) * hb - 1, 0), 0)),
            pl.BlockSpec((None, L, C), lambda d, b, i: (b, chunk(d, i), 0)),
            pl.BlockSpec((None, LRU_HALO, C),
                         lambda d, b, i: (b, jnp.minimum((chunk(d, i) + 1) * hb, S // LRU_HALO - 1), 0)),
            pl.BlockSpec((4, C), lambda d, b, i: (0, 0)),
            pl.BlockSpec((1, C), lambda d, b, i: (0, 0)),
            mat(), vec(), mat(), vec(), vec(),
        ],
        out_specs=pl.BlockSpec((None, None, L, C), lambda d, b, i: (d, b, chunk(d, i), 0)),
        out_shape=jax.ShapeDtypeStruct((2, B, S, C), F32),
        scratch_shapes=[pltpu.VMEM((L + 2 * LRU_HALO, C), F32), pltpu.VMEM((L, C), F32),
                        pltpu.VMEM((L, C), F32), pltpu.VMEM((1, C), F32)],
        compiler_params=_params("arbitrary", "arbitrary", "arbitrary"),
        name="rg_lru",
    )(rest, rest, rest, conv_w, conv_b.reshape(1, C), wa, ba.reshape(2, 1, C), wx, bx.reshape(2, 1, C),
      lam.reshape(2, 1, C))


def _merge_kernel(ya_ref, hf_ref, hb_ref, gr_ref, ga_ref, gb_ref, wa_ref, wr_ref, o_ref):
    yr = (hf_ref[...] + hb_ref[...]) * _gelu(gr_ref[...])
    pa = jnp.dot(ya_ref[...], wa_ref[...], preferred_element_type=F32)
    pr = jnp.dot(yr.astype(BF16), wr_ref[...], preferred_element_type=F32)
    o_ref[...] = (_sigmoid(ga_ref[...]) * pa + _sigmoid(gb_ref[...]) * pr).astype(o_ref.dtype)


def _merge(ya, hs, rest, wba, wbr, tm):
    T, C = ya.shape
    D = wba.shape[1]
    tm = min(tm, T)
    assert rest.shape[1] == 2 * C + 2 * D and D % C == 0
    return pl.pallas_call(
        _merge_kernel,
        grid=(T // tm,),
        in_specs=[
            pl.BlockSpec((tm, C), lambda i: (i, 0)),
            pl.BlockSpec((None, tm, C), lambda i: (0, i, 0)),
            pl.BlockSpec((None, tm, C), lambda i: (1, i, 0)),
            pl.BlockSpec((tm, C), lambda i: (i, 1)),
            pl.BlockSpec((tm, D), lambda i: (i, (2 * C) // D)),
            pl.BlockSpec((tm, D), lambda i: (i, (2 * C) // D + 1)),
            pl.BlockSpec((C, D), lambda i: (0, 0)),
            pl.BlockSpec((C, D), lambda i: (0, 0)),
        ],
        out_specs=pl.BlockSpec((tm, D), lambda i: (i, 0)),
        out_shape=jax.ShapeDtypeStruct((T, D), BF16),
        compiler_params=_params("parallel"),
        name="branch_merge",
    )(ya, hs, hs, rest, rest, rest, wba, wbr)


def _topk_rows(s, iota, n):
    vals, idxs = [], []
    for _ in range(PEER_TOPK):
        m = jnp.max(s, axis=0, keepdims=True)
        idx = jnp.min(jnp.where(s == m, iota, n), axis=0, keepdims=True)
        vals.append(m)
        idxs.append(idx)
        s = jnp.where(iota == idx, -jnp.inf, s)
    return jnp.concatenate(vals, axis=0), jnp.concatenate(idxs, axis=0)


def _route_kernel(h_ref, gn_ref, wq_ref, sk_ref, xn_ref, ids_ref, g_ref, q_s, ids_s, g_s, *, heads):
    tm = h_ref.shape[0]
    xn = _rms(h_ref[...], gn_ref[...])
    xn_ref[...] = xn
    q = jnp.dot(xn.astype(BF16), wq_ref[...], preferred_element_type=F32).astype(BF16)
    for hp in range(2 * heads):
        q_s[hp] = q[:, hp * PEER_NKEYS:(hp + 1) * PEER_NKEYS]
    K = PEER_TOPK
    iota_keys = lax.broadcasted_iota(I32, (PEER_NKEYS, tm), 0)
    iota_cand = lax.broadcasted_iota(I32, (K * K, tm), 0)
    iota_k = lax.broadcasted_iota(I32, (K, tm), 0)

    def head_body(h, carry):
        sv, si = [], []
        for p in range(2):
            s = lax.dot_general(sk_ref[2 * h + p], q_s[2 * h + p], (((1,), (1,)), ((), ())),
                                preferred_element_type=F32)
            v, ix = _topk_rows(s, iota_keys, PEER_NKEYS)
            sv.append(v)
            si.append(ix)
        cand = jnp.concatenate([sv[0][k1:k1 + 1, :] + sv[1] for k1 in range(K)], axis=0)
        cv, ci = _topk_rows(cand, iota_cand, K * K)
        ids = []
        for k in range(K):
            c = ci[k:k + 1, :]
            i1 = jnp.sum(jnp.where(iota_k == (c >> PEER_TOPK_LOG2), si[0], 0), axis=0, keepdims=True)
            i2 = jnp.sum(jnp.where(iota_k == (c & (K - 1)), si[1], 0), axis=0, keepdims=True)
            ids.append(i1 * PEER_NKEYS + i2)
        e = jnp.exp(cv - cv[0:1, :])
        row = pl.multiple_of(h * K, K)
        ids_s[pl.ds(row, K), :] = jnp.concatenate(ids, axis=0).astype(F32)
        g_s[pl.ds(row, K), :] = e / jnp.sum(e, axis=0, keepdims=True)
        return carry

    lax.fori_loop(0, heads, head_body, 0)
    ids_ref[...] = ids_s[...].T.astype(I32)
    g_ref[...] = g_s[...].T


def _route(h, gn, wq, sk, tm):
    T, D = h.shape
    NQ = wq.shape[1]
    heads = NQ // (2 * PEER_NKEYS)
    nsel = heads * PEER_TOPK
    tm = min(tm, T)
    return pl.pallas_call(
        functools.partial(_route_kernel, heads=heads),
        grid=(T // tm,),
        in_specs=[
            pl.BlockSpec((tm, D), lambda i: (i, 0)),
            pl.BlockSpec((1, D), lambda i: (0, 0)),
            pl.BlockSpec((D, NQ), lambda i: (0, 0)),
            pl.BlockSpec(sk.shape, lambda i: (0, 0, 0)),
        ],
        out_specs=[
            pl.BlockSpec((tm, D), lambda i: (i, 0)),
            pl.BlockSpec((tm, nsel), lambda i: (i, 0)),
            pl.BlockSpec((tm, nsel), lambda i: (i, 0)),
        ],
        out_shape=[
            jax.ShapeDtypeStruct((T, D), F32),
            jax.ShapeDtypeStruct((T, nsel), I32),
            jax.ShapeDtypeStruct((T, nsel), F32),
        ],
        scratch_shapes=[pltpu.VMEM((2 * heads, tm, PEER_NKEYS), BF16), pltpu.VMEM((nsel, tm), F32),
                        pltpu.VMEM((nsel, tm), F32)],
        compiler_params=_params("parallel"),
        name="peer_route",
    )(h, gn.reshape(1, D), wq, sk)


def _gather_kernel(ids_hbm, x_ref, g_ref, h_ref, tab_hbm, o_ref, ids_s, buf, sem, isem, *, tb, nbuf, nsel):
    i = pl.program_id(0)
    D = x_ref.shape[1]
    nlt = D // LANES
    ids_cp = pltpu.make_async_copy(
        ids_hbm.at[pl.ds(pl.multiple_of(i * tb * nsel, 1024), (tb + nbuf) * nsel)], ids_s, isem)
    ids_cp.start()
    ids_cp.wait()

    def issue(t, slot, n0, n1):
        base = t * nsel
        for n in range(n0, n1):
            e = ids_s[base + n]
            pltpu.make_async_copy(tab_hbm.at[pl.ds(e, 1), :], buf.at[slot, pl.ds(n, 1), :], sem.at[slot]).start()

    def wait(slot):
        pltpu.make_async_copy(tab_hbm.at[pl.ds(0, nsel), :], buf.at[slot], sem.at[slot]).wait()

    @pl.when(i == 0)
    def _():
        for s in range(nbuf - 1):
            issue(s, s, 0, nsel)

    eye = lax.broadcasted_iota(I32, (nsel, nsel), 0) == lax.broadcasted_iota(I32, (nsel, nsel), 1)
    per_chunk = nsel // (2 * nlt)
    assert per_chunk * 2 * nlt == nsel

    def group(gi, carry):
        for b in range(nbuf):
            t = gi * nbuf + b
            nt = t + (nbuf - 1)
            nslot = (b + nbuf - 1) % nbuf
            wait(b)
            xb = x_ref[pl.ds(t, 1), :]
            acc = None
            for j in range(nlt):
                cs = slice(j * LANES, (j + 1) * LANES)
                term = buf[b, :, cs] * xb[:, cs]
                acc = term if acc is None else acc + term
                issue(nt, nslot, j * per_chunk, (j + 1) * per_chunk)
            score = jnp.sum(acc, axis=1, keepdims=True)
            gcol = jnp.sum(jnp.where(eye, g_ref[pl.ds(t, 1), :], 0.0), axis=1, keepdims=True)
            w = gcol * _gelu(score)
            ys = []
            for j in range(nlt):
                ys.append(jnp.sum(buf[b, :, D + j * LANES:D + (j + 1) * LANES] * w, axis=0, keepdims=True))
                issue(nt, nslot, (nlt + j) * per_chunk, (nlt + j + 1) * per_chunk)
            o_ref[pl.ds(t, 1), :] = h_ref[pl.ds(t, 1), :] + jnp.concatenate(ys, axis=1)
        return carry

    lax.fori_loop(0, tb // nbuf, group, 0)

    @pl.when(i == pl.num_programs(0) - 1)
    def _():
        for s in range(nbuf - 1):
            wait(s)


def _gather(ids, xn, g, h, tab, tb, nbuf):
    T, D = xn.shape
    nsel = g.shape[1]
    tb = min(tb, T)
    assert tb % nbuf == 0 and T % tb == 0
    ids = jnp.concatenate([ids, jnp.zeros((nbuf * nsel,), I32)])
    return pl.pallas_call(
        functools.partial(_gather_kernel, tb=tb, nbuf=nbuf, nsel=nsel),
        grid=(T // tb,),
        in_specs=[
            pl.BlockSpec(memory_space=pl.ANY),
            pl.BlockSpec((tb, D), lambda i: (i, 0)),
            pl.BlockSpec((tb, nsel), lambda i: (i, 0)),
            pl.BlockSpec((tb, D), lambda i: (i, 0)),
            pl.BlockSpec(memory_space=pl.ANY),
        ],
        out_specs=pl.BlockSpec((tb, D), lambda i: (i, 0)),
        out_shape=jax.ShapeDtypeStruct((T, D), F32),
        scratch_shapes=[pltpu.SMEM(((tb + nbuf) * nsel,), I32), pltpu.VMEM((nbuf, nsel, 2 * D), F32),
                        pltpu.SemaphoreType.DMA((nbuf,)), pltpu.SemaphoreType.DMA(())],
        compiler_params=_params("arbitrary"),
        name="peer_gather",
    )(ids, xn, g, h, tab)


def _ple_kernel(h_ref, p_ref, gn_ref, wg_ref, wp_ref, gf_ref, o_ref):
    h = h_ref[...]
    hn = _rms(h, gn_ref[...]).astype(BF16)
    gate = _sigmoid(jnp.dot(hn, wg_ref[...], preferred_element_type=F32))
    pp = jnp.dot(p_ref[...].astype(BF16), wp_ref[...], preferred_element_type=F32)
    o_ref[...] = _rms(h + gate * pp, gf_ref[...])


def _ple(h, p, gn, wg, wp, gf, tm):
    T, D = h.shape
    P = p.shape[1]
    tm = min(tm, T)
    return pl.pallas_call(
        _ple_kernel,
        grid=(T // tm,),
        in_specs=[
            pl.BlockSpec((tm, D), lambda i: (i, 0)),
            pl.BlockSpec((tm, P), lambda i: (i, 0)),
            pl.BlockSpec((1, D), lambda i: (0, 0)),
            pl.BlockSpec((D, D), lambda i: (0, 0)),
            pl.BlockSpec((P, D), lambda i: (0, 0)),
            pl.BlockSpec((1, D), lambda i: (0, 0)),
        ],
        out_specs=pl.BlockSpec((tm, D), lambda i: (i, 0)),
        out_shape=jax.ShapeDtypeStruct((T, D), F32),
        compiler_params=_params("parallel"),
        name="ple_final",
    )(h, p, gn.reshape(1, D), wg, wp, gf.reshape(1, D))


def _trunk(x, p, w):
    B, S, D = x.shape
    T = B * S
    x2 = x.reshape(T, D)
    qkv = _norm_matmul(x2, w["norm_mix"], w["w_in_qkv"], BF16, 512, 1024)
    rest = _norm_matmul(x2, w["norm_mix"], w["w_in_rest"], F32, 512, 1024)
    ya = _na(qkv.reshape(B, S, -1), w["na_bias"])
    hs = _lru(rest.reshape(B, S, -1), w["conv_w"], w["conv_b"], w["lru_wa"], w["lru_ba"], w["lru_wx"],
              w["lru_bx"], w["lru_lambda"], 512)
    C = hs.shape[-1]
    merged = _merge(ya.reshape(T, -1), hs.reshape(2, T, C), rest, w["w_branch_a"], w["w_branch_r"], 256)
    h1 = _matmul_res(merged, w["w_out"], x2, 512, 1024)
    xn, ids, g = _route(h1, w["norm_ffn"], w["peer_wq"], w["peer_subkeys"], 256)
    h2 = _gather(ids.reshape(-1), xn, g, h1, w["peer_tab"], 128, 8)
    out = _ple(h2, p.reshape(T, -1), w["norm_ple"], w["ple_gate_w"], w["ple_proj_w"], w["final_norm"], 256)
    return out.reshape(B, S, D)


def kernel(x_prompt, x_sample, p_prompt, p_sample, norm_mix, w_in, na_rpb, conv_w, conv_b, lru_wa, lru_ba, lru_wx, lru_bx, lru_lambda, w_branch_a, w_branch_r, w_out, norm_ffn, peer_wq, peer_subkeys, peer_u, peer_v, norm_ple, ple_gate_w, ple_proj_w, final_norm):
    assert w_in.shape[0] == 1, "single layer"
    na_w = w_branch_a.shape[1]
    sk = peer_subkeys[0]
    w = dict(
        norm_mix=norm_mix[0],
        w_in_qkv=w_in[0][:, :3 * na_w].astype(BF16),
        w_in_rest=w_in[0][:, 3 * na_w:].astype(BF16),
        na_bias=_na_bias_table(na_rpb[0]),
        conv_w=conv_w[0], conv_b=conv_b[0],
        lru_wa=lru_wa[0].astype(BF16), lru_ba=lru_ba[0], lru_wx=lru_wx[0].astype(BF16), lru_bx=lru_bx[0],
        lru_lambda=lru_lambda[0],
        w_branch_a=w_branch_a[0].astype(BF16), w_branch_r=w_branch_r[0].astype(BF16),
        w_out=w_out[0].astype(BF16),
        norm_ffn=norm_ffn[0], peer_wq=peer_wq[0].astype(BF16),
        peer_subkeys=sk.reshape(sk.shape[0] * 2, sk.shape[2], sk.shape[3]).astype(BF16),
        peer_tab=jnp.concatenate([peer_u[0], peer_v[0]], axis=1),
        norm_ple=norm_ple[0], ple_gate_w=ple_gate_w[0].astype(BF16), ple_proj_w=ple_proj_w[0].astype(BF16),
        final_norm=final_norm,
    )
    return (_trunk(x_prompt, p_prompt[0], w), _trunk(x_sample, p_sample[0], w))
```

```python
import functools
import math

import numpy as np
import jax
import jax.numpy as jnp
from jax import lax
from jax.experimental import pallas as pl
from jax.experimental.pallas import tpu as pltpu

F32 = jnp.float32
BF16 = jnp.bfloat16
I32 = jnp.int32
U32 = jnp.uint32

LANES = 128
GRID_W = 64
NA_HEAD_DIM = 128
NA_KH = 8
NA_KW = 16
NA_ROWS_PER_STEP = 8
NEG_INF = -1e30
LRU_BLOCK = 128
LRU_C = 8.0
LRU_HALO = 8
PEER_NKEYS = 128
PEER_TOPK = 16
PEER_TOPK_LOG2 = 4
assert 1 << PEER_TOPK_LOG2 == PEER_TOPK
RMS_EPS = 1e-6
VMEM_LIMIT = 50 * 1024 * 1024


def _rms(x, g):
    return x * lax.rsqrt(jnp.mean(x * x, axis=-1, keepdims=True) + RMS_EPS) * g


def _gelu(x):
    return 0.5 * x * (1.0 + lax.erf(x * np.float32(math.sqrt(0.5))))


def _sigmoid(x):
    return jax.nn.sigmoid(x)


def _params(*sem):
    return pltpu.CompilerParams(dimension_semantics=sem, vmem_limit_bytes=VMEM_LIMIT)


def _norm_matmul_kernel(x_ref, g_ref, w_ref, o_ref, xn_ref):
    @pl.when(pl.program_id(1) == 0)
    def _():
        xn_ref[...] = _rms(x_ref[...], g_ref[...]).astype(BF16)

    o_ref[...] = jnp.dot(xn_ref[...], w_ref[...], preferred_element_type=F32).astype(o_ref.dtype)


def _norm_matmul(x, g, w, out_dtype, tm, tn):
    T, D = x.shape
    N = w.shape[1]
    tm, tn = min(tm, T), min(tn, N)
    return pl.pallas_call(
        _norm_matmul_kernel,
        grid=(T // tm, N // tn),
        in_specs=[
            pl.BlockSpec((tm, D), lambda i, j: (i, 0)),
            pl.BlockSpec((1, D), lambda i, j: (0, 0)),
            pl.BlockSpec((D, tn), lambda i, j: (0, j)),
        ],
        out_specs=pl.BlockSpec((tm, tn), lambda i, j: (i, j)),
        out_shape=jax.ShapeDtypeStruct((T, N), out_dtype),
        scratch_shapes=[pltpu.VMEM((tm, D), BF16)],
        compiler_params=_params("parallel", "arbitrary"),
        name="norm_matmul",
    )(x, g.reshape(1, D), w)


def _matmul_res_kernel(a_ref, w_ref, r_ref, o_ref):
    o_ref[...] = r_ref[...] + jnp.dot(a_ref[...], w_ref[...], preferred_element_type=F32)


def _matmul_res(a, w, res, tm, tn):
    T, K = a.shape
    N = w.shape[1]
    tm, tn = min(tm, T), min(tn, N)
    return pl.pallas_call(
        _matmul_res_kernel,
        grid=(T // tm, N // tn),
        in_specs=[
            pl.BlockSpec((tm, K), lambda i, j: (i, 0)),
            pl.BlockSpec((K, tn), lambda i, j: (0, j)),
            pl.BlockSpec((tm, tn), lambda i, j: (i, j)),
        ],
        out_specs=pl.BlockSpec((tm, tn), lambda i, j: (i, j)),
        out_shape=jax.ShapeDtypeStruct((T, N), F32),
        compiler_params=_params("parallel", "arbitrary"),
        name="matmul_res",
    )(a, w, res)


def _na_bias_table(rpb):
    qc = np.arange(GRID_W)
    kc = np.arange(GRID_W)
    col_start = np.clip(qc - NA_KW // 2, 0, GRID_W - NA_KW)
    ok = (kc[None, :] >= col_start[:, None]) & (kc[None, :] < col_start[:, None] + NA_KW)
    dc = kc[None, :] - qc[:, None] + (NA_KW - 1)
    onehot = ((dc[None] == np.arange(2 * NA_KW - 1)[:, None, None]) & ok[None]).astype(np.float32)
    t = jnp.einsum("hrd,dqk->hrqk", rpb.astype(F32), onehot, precision=lax.Precision.HIGHEST)
    t = jnp.where(ok[None, None], t, NEG_INF)
    per_off = [jnp.concatenate([t[:, slot - off + NA_KH - 1] for slot in range(NA_KH)], axis=-1)
               for off in range(NA_KH)]
    return jnp.stack(per_off, axis=1)


def _na_kernel(q_ref, kp_ref, kc_ref, kn_ref, vp_ref, vc_ref, vn_ref, b_ref, o_ref, kbuf, vbuf,
               *, rows, heads):
    i = pl.program_id(1)
    tb = NA_ROWS_PER_STEP * GRID_W
    nkeys = NA_KH * GRID_W
    kbuf[0:tb] = kp_ref[...]
    kbuf[tb:2 * tb] = kc_ref[...]
    kbuf[2 * tb:3 * tb] = kn_ref[...]
    vbuf[0:tb] = vp_ref[...]
    vbuf[tb:2 * tb] = vc_ref[...]
    vbuf[2 * tb:3 * tb] = vn_ref[...]
    scale = np.float32(NA_HEAD_DIM ** -0.5)

    def row_body(rr, carry):
        r = i * NA_ROWS_PER_STEP + rr
        rs = jnp.clip(r - NA_KH // 2, 0, rows - NA_KH)
        woff = r - rs
        koff = pl.multiple_of((rs - (i - 1) * NA_ROWS_PER_STEP) * GRID_W, GRID_W)
        qoff = pl.multiple_of(rr * GRID_W, GRID_W)
        for h in range(heads):
            cs = slice(h * NA_HEAD_DIM, (h + 1) * NA_HEAD_DIM)
            q = q_ref[pl.ds(qoff, GRID_W), cs]
            k = kbuf[pl.ds(koff, nkeys), cs]
            v = vbuf[pl.ds(koff, nkeys), cs]
            s = lax.dot_general(q, k, (((1,), (1,)), ((), ())), preferred_element_type=F32)
            s = s * scale + b_ref[h, woff]
            m = jnp.max(s, axis=-1, keepdims=True)
            p = jnp.exp(s - m)
            l = jnp.sum(p, axis=-1, keepdims=True)
            out = jnp.dot(p.astype(BF16), v, preferred_element_type=F32) / l
            o_ref[pl.ds(qoff, GRID_W), cs] = out.astype(o_ref.dtype)
        return carry

    lax.fori_loop(0, NA_ROWS_PER_STEP, row_body, 0)


def _na(qkv, bias):
    B, S, W3 = qkv.shape
    W = W3 // 3
    heads = W // NA_HEAD_DIM
    rows = S // GRID_W
    assert rows % NA_ROWS_PER_STEP == 0 and rows >= NA_KH
    nblk = rows // NA_ROWS_PER_STEP
    tb = NA_ROWS_PER_STEP * GRID_W

    def spec(col, shift):
        return pl.BlockSpec((None, tb, W), lambda b, i: (b, jnp.clip(i + shift, 0, nblk - 1), col))

    return pl.pallas_call(
        functools.partial(_na_kernel, rows=rows, heads=heads),
        grid=(B, nblk),
        in_specs=[spec(0, 0), spec(1, -1), spec(1, 0), spec(1, 1), spec(2, -1), spec(2, 0), spec(2, 1),
                  pl.BlockSpec(bias.shape, lambda b, i: (0, 0, 0, 0))],
        out_specs=pl.BlockSpec((None, tb, W), lambda b, i: (b, i, 0)),
        out_shape=jax.ShapeDtypeStruct((B, S, W), BF16),
        scratch_shapes=[pltpu.VMEM((3 * tb, W), BF16), pltpu.VMEM((3 * tb, W), BF16)],
        compiler_params=_params("parallel", "arbitrary"),
        name="na_attention",
    )(qkv, qkv, qkv, qkv, qkv, qkv, qkv, bias)


def _lru_kernel(xp_ref, x_ref, xn_ref, cw_ref, cb_ref, wa_ref, ba_ref, wx_ref, bx_ref, lam_ref,
                o_ref, xe, a_s, u_s, carry, *, L, nch, nblk):
    d = pl.program_id(0)
    i = pl.program_id(2)
    c = jnp.where(d == 0, i, nch - 1 - i)

    @pl.when(i == 0)
    def _():
        carry[...] = jnp.zeros_like(carry)

    xe[0:LRU_HALO] = jnp.where(c > 0, xp_ref[...], 0.0)
    xe[LRU_HALO:LRU_HALO + L] = x_ref[...]
    xe[LRU_HALO + L:2 * LRU_HALO + L] = jnp.where(c < nch - 1, xn_ref[...], 0.0)
    cw = cw_ref[...]
    xc = cb_ref[...] + cw[2:3] * xe[pl.ds(LRU_HALO, L), :]
    xc = xc + cw[0:1] * xe[pl.ds(LRU_HALO - 2, L), :]
    xc = xc + cw[1:2] * xe[pl.ds(LRU_HALO - 1, L), :]
    xc = xc + cw[3:4] * xe[pl.ds(LRU_HALO + 1, L), :]
    xcb = xc.astype(BF16)

    def blockdiag(w_ref):
        outs = []
        for n in range(nblk):
            cs = slice(n * LRU_BLOCK, (n + 1) * LRU_BLOCK)
            outs.append(jnp.dot(xcb[:, cs], w_ref[n], preferred_element_type=F32))
        return jnp.concatenate(outs, axis=1)

    rec = _sigmoid(blockdiag(wa_ref) + ba_ref[...])
    inp = _sigmoid(blockdiag(wx_ref) + bx_ref[...])
    lam = lam_ref[...]
    softplus_neg_lam = jnp.maximum(-lam, 0.0) + jnp.log1p(jnp.exp(-jnp.abs(lam)))
    log_a = -LRU_C * rec * softplus_neg_lam
    a_s[...] = jnp.exp(log_a)
    y2 = 2.0 * log_a
    e2 = jnp.exp(y2)
    em1 = e2 - 1.0
    expm1 = jnp.where(em1 == 0.0, y2, jnp.where(e2 == 0.0, -1.0, em1 * y2 / jnp.log(e2)))
    u_s[...] = jnp.sqrt(-expm1) * (inp * xc)

    def step(t, h):
        tt = jnp.where(d == 0, t, L - 1 - t)
        h = a_s[pl.ds(tt, 1), :] * h + u_s[pl.ds(tt, 1), :]
        o_ref[pl.ds(tt, 1), :] = h
        return h

    carry[...] = lax.fori_loop(0, L, step, carry[...], unroll=8)


def _lru(rest, conv_w, conv_b, wa, ba, wx, bx, lam, L):
    B, S, _ = rest.shape
    C = conv_w.shape[1]
    nblk = C // LRU_BLOCK
    L = min(L, S)
    nch = S // L
    hb = L // LRU_HALO

    def chunk(d, i):
        return jnp.where(d == 0, i, nch - 1 - i)

    vec = lambda: pl.BlockSpec((None, 1, C), lambda d, b, i: (d, 0, 0))
    mat = lambda: pl.BlockSpec((None, nblk, LRU_BLOCK, LRU_BLOCK), lambda d, b, i: (d, 0, 0, 0))
    return pl.pallas_call(
        functools.partial(_lru_kernel, L=L, nch=nch, nblk=nblk),
        grid=(2, B, nch),
        in_specs=[
            pl.BlockSpec((None, LRU_HALO, C), lambda d, b, i: (b, jnp.maximum(chunk(d, i) * hb - 1, 0), 0)),
            pl.BlockSpec((None, L, C), lambda d, b, i: (b, chunk(d, i), 0)),
            pl.BlockSpec((None, LRU_HALO, C),
                         lambda d, b, i: (b, jnp.minimum((chunk(d, i) + 1) * hb, S // LRU_HALO - 1), 0)),
            pl.BlockSpec((4, C), lambda d, b, i: (0, 0)),
            pl.BlockSpec((1, C), lambda d, b, i: (0, 0)),
            mat(), vec(), mat(), vec(), vec(),
        ],
        out_specs=pl.BlockSpec((None, None, L, C), lambda d, b, i: (d, b, chunk(d, i), 0)),
        out_shape=jax.ShapeDtypeStruct((2, B, S, C), F32),
        scratch_shapes=[pltpu.VMEM((L + 2 * LRU_HALO, C), F32), pltpu.VMEM((L, C), F32),
                        pltpu.VMEM((L, C), F32), pltpu.VMEM((1, C), F32)],
        compiler_params=_params("arbitrary", "arbitrary", "arbitrary"),
        name="rg_lru",
    )(rest, rest, rest, conv_w, conv_b.reshape(1, C), wa, ba.reshape(2, 1, C), wx, bx.reshape(2, 1, C),
      lam.reshape(2, 1, C))


def _merge_kernel(ya_ref, hf_ref, hb_ref, gr_ref, ga_ref, gb_ref, wa_ref, wr_ref, o_ref):
    yr = (hf_ref[...] + hb_ref[...]) * _gelu(gr_ref[...])
    pa = jnp.dot(ya_ref[...], wa_ref[...], preferred_element_type=F32)
    pr = jnp.dot(yr.astype(BF16), wr_ref[...], preferred_element_type=F32)
    o_ref[...] = (_sigmoid(ga_ref[...]) * pa + _sigmoid(gb_ref[...]) * pr).astype(o_ref.dtype)


def _merge(ya, hs, rest, wba, wbr, tm):
    T, C = ya.shape
    D = wba.shape[1]
    tm = min(tm, T)
    assert rest.shape[1] == 2 * C + 2 * D and D % C == 0
    return pl.pallas_call(
        _merge_kernel,
        grid=(T // tm,),
        in_specs=[
            pl.BlockSpec((tm, C), lambda i: (i, 0)),
            pl.BlockSpec((None, tm, C), lambda i: (0, i, 0)),
            pl.BlockSpec((None, tm, C), lambda i: (1, i, 0)),
            pl.BlockSpec((tm, C), lambda i: (i, 1)),
            pl.BlockSpec((tm, D), lambda i: (i, (2 * C) // D)),
            pl.BlockSpec((tm, D), lambda i: (i, (2 * C) // D + 1)),
            pl.BlockSpec((C, D), lambda i: (0, 0)),
            pl.BlockSpec((C, D), lambda i: (0, 0)),
        ],
        out_specs=pl.BlockSpec((tm, D), lambda i: (i, 0)),
        out_shape=jax.ShapeDtypeStruct((T, D), BF16),
        compiler_params=_params("parallel"),
        name="branch_merge",
    )(ya, hs, hs, rest, rest, rest, wba, wbr)


def _topk_rows(s, iota, n):
    vals, idxs = [], []
    for _ in range(PEER_TOPK):
        m = jnp.max(s, axis=0, keepdims=True)
        idx = jnp.min(jnp.where(s == m, iota, n), axis=0, keepdims=True)
        vals.append(m)
        idxs.append(idx)
        s = jnp.where(iota == idx, -jnp.inf, s)
    return jnp.concatenate(vals, axis=0), jnp.concatenate(idxs, axis=0)


def _route_kernel(h_ref, gn_ref, wq_ref, sk_ref, xn_ref, ids_ref, g_ref, q_s, ids_s, g_s, *, heads):
    tm = h_ref.shape[0]
    xn = _rms(h_ref[...], gn_ref[...])
    xn_ref[...] = xn
    q = jnp.dot(xn.astype(BF16), wq_ref[...], preferred_element_type=F32).astype(BF16)
    for hp in range(2 * heads):
        q_s[hp] = q[:, hp * PEER_NKEYS:(hp + 1) * PEER_NKEYS]
    K = PEER_TOPK
    iota_keys = lax.broadcasted_iota(I32, (PEER_NKEYS, tm), 0)
    iota_k = lax.broadcasted_iota(I32, (K, tm), 0)
    ka = 4
    kb = K // (ka + 1)
    cand_idx = jnp.concatenate([k1 * K + iota_k for k1 in range(ka)] + [iota_k * K + k2 for k2 in range(kb)], axis=0)

    def head_body(h, carry):
        sv, si = [], []
        for p in range(2):
            s = lax.dot_general(sk_ref[2 * h + p], q_s[2 * h + p], (((1,), (1,)), ((), ())),
                                preferred_element_type=F32)
            v, ix = _topk_rows(s, iota_keys, PEER_NKEYS)
            sv.append(v)
            si.append(ix)
        cand = jnp.concatenate(
            [sv[0][k1:k1 + 1, :] + sv[1] for k1 in range(ka)]
            + [jnp.where(iota_k >= ka, sv[0] + sv[1][k2:k2 + 1, :], -jnp.inf) for k2 in range(kb)], axis=0)
        cv, ci = _topk_rows(cand, cand_idx, K * K)
        ids = []
        for k in range(K):
            c = ci[k:k + 1, :]
            i1 = jnp.sum(jnp.where(iota_k == (c >> PEER_TOPK_LOG2), si[0], 0), axis=0, keepdims=True)
            i2 = jnp.sum(jnp.where(iota_k == (c & (K - 1)), si[1], 0), axis=0, keepdims=True)
            ids.append(i1 * PEER_NKEYS + i2)
        e = jnp.exp(cv - cv[0:1, :])
        row = pl.multiple_of(h * K, K)
        ids_s[pl.ds(row, K), :] = jnp.concatenate(ids, axis=0).astype(F32)
        g_s[pl.ds(row, K), :] = e / jnp.sum(e, axis=0, keepdims=True)
        return carry

    lax.fori_loop(0, heads, head_body, 0)
    ids_ref[...] = ids_s[...].T.astype(I32)
    g_ref[...] = g_s[...].T


def _route(h, gn, wq, sk, tm):
    T, D = h.shape
    NQ = wq.shape[1]
    heads = NQ // (2 * PEER_NKEYS)
    nsel = heads * PEER_TOPK
    tm = min(tm, T)
    return pl.pallas_call(
        functools.partial(_route_kernel, heads=heads),
        grid=(T // tm,),
        in_specs=[
            pl.BlockSpec((tm, D), lambda i: (i, 0)),
            pl.BlockSpec((1, D), lambda i: (0, 0)),
            pl.BlockSpec((D, NQ), lambda i: (0, 0)),
            pl.BlockSpec(sk.shape, lambda i: (0, 0, 0)),
        ],
        out_specs=[
            pl.BlockSpec((tm, D), lambda i: (i, 0)),
            pl.BlockSpec((tm, nsel), lambda i: (i, 0)),
            pl.BlockSpec((tm, nsel), lambda i: (i, 0)),
        ],
        out_shape=[
            jax.ShapeDtypeStruct((T, D), F32),
            jax.ShapeDtypeStruct((T, nsel), I32),
            jax.ShapeDtypeStruct((T, nsel), F32),
        ],
        scratch_shapes=[pltpu.VMEM((2 * heads, tm, PEER_NKEYS), BF16), pltpu.VMEM((nsel, tm), F32),
                        pltpu.VMEM((nsel, tm), F32)],
        compiler_params=_params("parallel"),
        name="peer_route",
    )(h, gn.reshape(1, D), wq, sk)


def _pack_bf16_pairs(lo, hi):
    lo16 = lax.bitcast_convert_type(lo.astype(BF16), jnp.uint16).astype(U32)
    hi16 = lax.bitcast_convert_type(hi.astype(BF16), jnp.uint16).astype(U32)
    return lo16 | (hi16 << 16)


def _unpack_lo(words):
    return lax.bitcast_convert_type(words << 16, F32)


def _unpack_hi(words):
    return lax.bitcast_convert_type(words & np.uint32(0xFFFF0000), F32)


def _gather_kernel(ids_hbm, x_ref, g_ref, h_ref, tab_hbm, o_ref, ids_s, sem, isem, *bufs, tb, nbuf, nsel):
    i = pl.program_id(0)
    D = x_ref.shape[1]
    nlt = D // LANES
    ids_cp = pltpu.make_async_copy(
        ids_hbm.at[pl.ds(pl.multiple_of(i * tb * nsel, 1024), (tb + nbuf) * nsel)], ids_s, isem)
    ids_cp.start()
    ids_cp.wait()

    def issue(t, slot, n0, n1):
        base = t * nsel
        for n in range(n0, n1):
            e = ids_s[base + n]
            pltpu.make_async_copy(tab_hbm.at[e], bufs[slot].at[pl.ds(n, 1), :], sem.at[slot]).start()

    def wait(slot):
        pltpu.make_async_copy(bufs[slot], bufs[slot], sem.at[slot]).wait()

    @pl.when(i == 0)
    def _():
        for s in range(nbuf - 1):
            issue(s, s, 0, nsel)

    eye = lax.broadcasted_iota(I32, (nsel, nsel), 0) == lax.broadcasted_iota(I32, (nsel, nsel), 1)
    per_chunk = nsel // (2 * nlt)
    assert per_chunk * 2 * nlt == nsel

    def group(gi, carry):
        for b in range(nbuf):
            t = gi * nbuf + b
            nt = t + (nbuf - 1)
            nslot = (b + nbuf - 1) % nbuf
            wait(b)
            xb = x_ref[pl.ds(t, 1), :]
            acc = None
            for j in range(nlt):
                cs = slice(j * LANES, (j + 1) * LANES)
                term = _unpack_lo(bufs[b][:, cs]) * xb[:, cs]
                acc = term if acc is None else acc + term
                issue(nt, nslot, j * per_chunk, (j + 1) * per_chunk)
            score = jnp.sum(acc, axis=1, keepdims=True)
            gcol = jnp.sum(jnp.where(eye, g_ref[pl.ds(t, 1), :], 0.0), axis=1, keepdims=True)
            w = gcol * _gelu(score)
            ys = []
            for j in range(nlt):
                ys.append(jnp.sum(_unpack_hi(bufs[b][:, j * LANES:(j + 1) * LANES]) * w, axis=0, keepdims=True))
                issue(nt, nslot, (nlt + j) * per_chunk, (nlt + j + 1) * per_chunk)
            o_ref[pl.ds(t, 1), :] = h_ref[pl.ds(t, 1), :] + jnp.concatenate(ys, axis=1)
        return carry

    lax.fori_loop(0, tb // nbuf, group, 0)

    @pl.when(i == pl.num_programs(0) - 1)
    def _():
        for s in range(nbuf - 1):
            wait(s)


def _gather(ids, xn, g, h, tab, tb, nbuf):
    T, D = xn.shape
    nsel = g.shape[1]
    tb = min(tb, T)
    assert tb % nbuf == 0 and T % tb == 0
    ids = jnp.concatenate([ids, jnp.zeros((nbuf * nsel,), I32)])
    return pl.pallas_call(
        functools.partial(_gather_kernel, tb=tb, nbuf=nbuf, nsel=nsel),
        grid=(T // tb,),
        in_specs=[
            pl.BlockSpec(memory_space=pl.ANY),
            pl.BlockSpec((tb, D), lambda i: (i, 0)),
            pl.BlockSpec((tb, nsel), lambda i: (i, 0)),
            pl.BlockSpec((tb, D), lambda i: (i, 0)),
            pl.BlockSpec(memory_space=pl.ANY),
        ],
        out_specs=pl.BlockSpec((tb, D), lambda i: (i, 0)),
        out_shape=jax.ShapeDtypeStruct((T, D), F32),
        scratch_shapes=[pltpu.SMEM(((tb + nbuf) * nsel,), I32), pltpu.SemaphoreType.DMA((nbuf,)),
                        pltpu.SemaphoreType.DMA(())] + [pltpu.VMEM((nsel, D), U32) for _ in range(nbuf)],
        compiler_params=_params("arbitrary"),
        name="peer_gather",
    )(ids, xn, g, h, tab.reshape(tab.shape[0], 1, D))


def _ple_kernel(h_ref, p_ref, gn_ref, wg_ref, wp_ref, gf_ref, o_ref):
    h = h_ref[...]
    hn = _rms(h, gn_ref[...]).astype(BF16)
    gate = _sigmoid(jnp.dot(hn, wg_ref[...], preferred_element_type=F32))
    pp = jnp.dot(p_ref[...].astype(BF16), wp_ref[...], preferred_element_type=F32)
    o_ref[...] = _rms(h + gate * pp, gf_ref[...])


def _ple(h, p, gn, wg, wp, gf, tm):
    T, D = h.shape
    P = p.shape[1]
    tm = min(tm, T)
    return pl.pallas_call(
        _ple_kernel,
        grid=(T // tm,),
        in_specs=[
            pl.BlockSpec((tm, D), lambda i: (i, 0)),
            pl.BlockSpec((tm, P), lambda i: (i, 0)),
            pl.BlockSpec((1, D), lambda i: (0, 0)),
            pl.BlockSpec((D, D), lambda i: (0, 0)),
            pl.BlockSpec((P, D), lambda i: (0, 0)),
            pl.BlockSpec((1, D), lambda i: (0, 0)),
        ],
        out_specs=pl.BlockSpec((tm, D), lambda i: (i, 0)),
        out_shape=jax.ShapeDtypeStruct((T, D), F32),
        compiler_params=_params("parallel"),
        name="ple_final",
    )(h, p, gn.reshape(1, D), wg, wp, gf.reshape(1, D))


def _trunk(x, p, w):
    B, S, D = x.shape
    T = B * S
    x2 = x.reshape(T, D)
    qkv = _norm_matmul(x2, w["norm_mix"], w["w_in_qkv"], BF16, 512, 1024)
    rest = _norm_matmul(x2, w["norm_mix"], w["w_in_rest"], F32, 512, 1024)
    ya = _na(qkv.reshape(B, S, -1), w["na_bias"])
    hs = _lru(rest.reshape(B, S, -1), w["conv_w"], w["conv_b"], w["lru_wa"], w["lru_ba"], w["lru_wx"],
              w["lru_bx"], w["lru_lambda"], 512)
    C = hs.shape[-1]
    merged = _merge(ya.reshape(T, -1), hs.reshape(2, T, C), rest, w["w_branch_a"], w["w_branch_r"], 256)
    h1 = _matmul_res(merged, w["w_out"], x2, 512, 1024)
    xn, ids, g = _route(h1, w["norm_ffn"], w["peer_wq"], w["peer_subkeys"], 256)
    h2 = _gather(ids.reshape(-1), xn, g, h1, w["peer_tab"], 128, 8)
    out = _ple(h2, p.reshape(T, -1), w["norm_ple"], w["ple_gate_w"], w["ple_proj_w"], w["final_norm"], 256)
    return out.reshape(B, S, D)


def kernel(x_prompt, x_sample, p_prompt, p_sample, norm_mix, w_in, na_rpb, conv_w, conv_b, lru_wa, lru_ba, lru_wx, lru_bx, lru_lambda, w_branch_a, w_branch_r, w_out, norm_ffn, peer_wq, peer_subkeys, peer_u, peer_v, norm_ple, ple_gate_w, ple_proj_w, final_norm):
    assert w_in.shape[0] == 1, "single layer"
    na_w = w_branch_a.shape[1]
    sk = peer_subkeys[0]
    w = dict(
        norm_mix=norm_mix[0],
        w_in_qkv=w_in[0][:, :3 * na_w].astype(BF16),
        w_in_rest=w_in[0][:, 3 * na_w:].astype(BF16),
        na_bias=_na_bias_table(na_rpb[0]),
        conv_w=conv_w[0], conv_b=conv_b[0],
        lru_wa=lru_wa[0].astype(BF16), lru_ba=lru_ba[0], lru_wx=lru_wx[0].astype(BF16), lru_bx=lru_bx[0],
        lru_lambda=lru_lambda[0],
        w_branch_a=w_branch_a[0].astype(BF16), w_branch_r=w_branch_r[0].astype(BF16),
        w_out=w_out[0].astype(BF16),
        norm_ffn=norm_ffn[0], peer_wq=peer_wq[0].astype(BF16),
        peer_subkeys=sk.reshape(sk.shape[0] * 2, sk.shape[2], sk.shape[3]).astype(BF16),
        peer_tab=_pack_bf16_pairs(peer_u[0], peer_v[0]),
        norm_ple=norm_ple[0], ple_gate_w=ple_gate_w[0].astype(BF16), ple_proj_w=ple_proj_w[0].astype(BF16),
        final_norm=final_norm,
    )
    return (_trunk(x_prompt, p_prompt[0], w), _trunk(x_sample, p_sample[0], w))
```

```python
import functools
import math

import numpy as np
import jax
import jax.numpy as jnp
from jax import lax
from jax.experimental import pallas as pl
from jax.experimental.pallas import tpu as pltpu

F32 = jnp.float32
BF16 = jnp.bfloat16
I32 = jnp.int32
U32 = jnp.uint32

LANES = 128
GRID_W = 64
NA_HEAD_DIM = 128
NA_KH = 8
NA_KW = 16
NA_ROWS_PER_STEP = 8
NEG_INF = -1e30
LRU_BLOCK = 128
LRU_C = 8.0
LRU_HALO = 8
PEER_NKEYS = 128
PEER_TOPK = 16
PEER_TOPK_LOG2 = 4
assert 1 << PEER_TOPK_LOG2 == PEER_TOPK
RMS_EPS = 1e-6
VMEM_LIMIT = 50 * 1024 * 1024


def _rms(x, g):
    return x * lax.rsqrt(jnp.mean(x * x, axis=-1, keepdims=True) + RMS_EPS) * g


def _gelu(x):
    return 0.5 * x * (1.0 + lax.erf(x * np.float32(math.sqrt(0.5))))


def _sigmoid(x):
    return jax.nn.sigmoid(x)


def _params(*sem):
    return pltpu.CompilerParams(dimension_semantics=sem, vmem_limit_bytes=VMEM_LIMIT)


def _norm_matmul_kernel(x_ref, g_ref, w_ref, o_ref, xn_ref):
    @pl.when(pl.program_id(1) == 0)
    def _():
        xn_ref[...] = _rms(x_ref[...], g_ref[...]).astype(BF16)

    o_ref[...] = jnp.dot(xn_ref[...], w_ref[...], preferred_element_type=F32).astype(o_ref.dtype)


def _norm_matmul(x, g, w, out_dtype, tm, tn):
    T, D = x.shape
    N = w.shape[1]
    tm, tn = min(tm, T), min(tn, N)
    return pl.pallas_call(
        _norm_matmul_kernel,
        grid=(T // tm, N // tn),
        in_specs=[
            pl.BlockSpec((tm, D), lambda i, j: (i, 0)),
            pl.BlockSpec((1, D), lambda i, j: (0, 0)),
            pl.BlockSpec((D, tn), lambda i, j: (0, j)),
        ],
        out_specs=pl.BlockSpec((tm, tn), lambda i, j: (i, j)),
        out_shape=jax.ShapeDtypeStruct((T, N), out_dtype),
        scratch_shapes=[pltpu.VMEM((tm, D), BF16)],
        compiler_params=_params("parallel", "arbitrary"),
        name="norm_matmul",
    )(x, g.reshape(1, D), w)


def _matmul_res_kernel(a_ref, w_ref, r_ref, o_ref):
    o_ref[...] = r_ref[...] + jnp.dot(a_ref[...], w_ref[...], preferred_element_type=F32)


def _matmul_res(a, w, res, tm, tn):
    T, K = a.shape
    N = w.shape[1]
    tm, tn = min(tm, T), min(tn, N)
    return pl.pallas_call(
        _matmul_res_kernel,
        grid=(T // tm, N // tn),
        in_specs=[
            pl.BlockSpec((tm, K), lambda i, j: (i, 0)),
            pl.BlockSpec((K, tn), lambda i, j: (0, j)),
            pl.BlockSpec((tm, tn), lambda i, j: (i, j)),
        ],
        out_specs=pl.BlockSpec((tm, tn), lambda i, j: (i, j)),
        out_shape=jax.ShapeDtypeStruct((T, N), F32),
        compiler_params=_params("parallel", "arbitrary"),
        name="matmul_res",
    )(a, w, res)


def _na_bias_table(rpb):
    qc = np.arange(GRID_W)
    kc = np.arange(GRID_W)
    col_start = np.clip(qc - NA_KW // 2, 0, GRID_W - NA_KW)
    ok = (kc[None, :] >= col_start[:, None]) & (kc[None, :] < col_start[:, None] + NA_KW)
    dc = kc[None, :] - qc[:, None] + (NA_KW - 1)
    onehot = ((dc[None] == np.arange(2 * NA_KW - 1)[:, None, None]) & ok[None]).astype(np.float32)
    t = jnp.einsum("hrd,dqk->hrqk", rpb.astype(F32), onehot, precision=lax.Precision.HIGHEST)
    t = jnp.where(ok[None, None], t, NEG_INF)
    per_off = [jnp.concatenate([t[:, slot - off + NA_KH - 1] for slot in range(NA_KH)], axis=-1)
               for off in range(NA_KH)]
    return jnp.stack(per_off, axis=1)


def _na_kernel(q_ref, kp_ref, kc_ref, kn_ref, vp_ref, vc_ref, vn_ref, b_ref, o_ref, kbuf, vbuf,
               *, rows, heads):
    i = pl.program_id(1)
    tb = NA_ROWS_PER_STEP * GRID_W
    nkeys = NA_KH * GRID_W
    kbuf[0:tb] = kp_ref[...]
    kbuf[tb:2 * tb] = kc_ref[...]
    kbuf[2 * tb:3 * tb] = kn_ref[...]
    vbuf[0:tb] = vp_ref[...]
    vbuf[tb:2 * tb] = vc_ref[...]
    vbuf[2 * tb:3 * tb] = vn_ref[...]
    scale = np.float32(NA_HEAD_DIM ** -0.5)

    def row_body(rr, carry):
        r = i * NA_ROWS_PER_STEP + rr
        rs = jnp.clip(r - NA_KH // 2, 0, rows - NA_KH)
        woff = r - rs
        koff = pl.multiple_of((rs - (i - 1) * NA_ROWS_PER_STEP) * GRID_W, GRID_W)
        qoff = pl.multiple_of(rr * GRID_W, GRID_W)
        for h in range(heads):
            cs = slice(h * NA_HEAD_DIM, (h + 1) * NA_HEAD_DIM)
            q = q_ref[pl.ds(qoff, GRID_W), cs]
            k = kbuf[pl.ds(koff, nkeys), cs]
            v = vbuf[pl.ds(koff, nkeys), cs]
            s = lax.dot_general(q, k, (((1,), (1,)), ((), ())), preferred_element_type=F32)
            s = s * scale + b_ref[h, woff]
            m = jnp.max(s, axis=-1, keepdims=True)
            p = jnp.exp(s - m)
            l = jnp.sum(p, axis=-1, keepdims=True)
            out = jnp.dot(p.astype(BF16), v, preferred_element_type=F32) / l
            o_ref[pl.ds(qoff, GRID_W), cs] = out.astype(o_ref.dtype)
        return carry

    lax.fori_loop(0, NA_ROWS_PER_STEP, row_body, 0)


def _na(qkv, bias):
    B, S, W3 = qkv.shape
    W = W3 // 3
    heads = W // NA_HEAD_DIM
    rows = S // GRID_W
    assert rows % NA_ROWS_PER_STEP == 0 and rows >= NA_KH
    nblk = rows // NA_ROWS_PER_STEP
    tb = NA_ROWS_PER_STEP * GRID_W

    def spec(col, shift):
        return pl.BlockSpec((None, tb, W), lambda b, i: (b, jnp.clip(i + shift, 0, nblk - 1), col))

    return pl.pallas_call(
        functools.partial(_na_kernel, rows=rows, heads=heads),
        grid=(B, nblk),
        in_specs=[spec(0, 0), spec(1, -1), spec(1, 0), spec(1, 1), spec(2, -1), spec(2, 0), spec(2, 1),
                  pl.BlockSpec(bias.shape, lambda b, i: (0, 0, 0, 0))],
        out_specs=pl.BlockSpec((None, tb, W), lambda b, i: (b, i, 0)),
        out_shape=jax.ShapeDtypeStruct((B, S, W), BF16),
        scratch_shapes=[pltpu.VMEM((3 * tb, W), BF16), pltpu.VMEM((3 * tb, W), BF16)],
        compiler_params=_params("parallel", "arbitrary"),
        name="na_attention",
    )(qkv, qkv, qkv, qkv, qkv, qkv, qkv, bias)


def _lru_kernel(xp_ref, x_ref, xn_ref, cw_ref, cb_ref, wa_ref, ba_ref, wx_ref, bx_ref, lam_ref,
                o_ref, xe, a_s, u_s, carry, *, L, nch, nblk):
    d = pl.program_id(0)
    i = pl.program_id(2)
    c = jnp.where(d == 0, i, nch - 1 - i)

    @pl.when(i == 0)
    def _():
        carry[...] = jnp.zeros_like(carry)

    xe[0:LRU_HALO] = jnp.where(c > 0, xp_ref[...], 0.0)
    xe[LRU_HALO:LRU_HALO + L] = x_ref[...]
    xe[LRU_HALO + L:2 * LRU_HALO + L] = jnp.where(c < nch - 1, xn_ref[...], 0.0)
    cw = cw_ref[...]
    xc = cb_ref[...] + cw[2:3] * xe[pl.ds(LRU_HALO, L), :]
    xc = xc + cw[0:1] * xe[pl.ds(LRU_HALO - 2, L), :]
    xc = xc + cw[1:2] * xe[pl.ds(LRU_HALO - 1, L), :]
    xc = xc + cw[3:4] * xe[pl.ds(LRU_HALO + 1, L), :]
    xcb = xc.astype(BF16)

    def blockdiag(w_ref):
        outs = []
        for n in range(nblk):
            cs = slice(n * LRU_BLOCK, (n + 1) * LRU_BLOCK)
            outs.append(jnp.dot(xcb[:, cs], w_ref[n], preferred_element_type=F32))
        return jnp.concatenate(outs, axis=1)

    rec = _sigmoid(blockdiag(wa_ref) + ba_ref[...])
    inp = _sigmoid(blockdiag(wx_ref) + bx_ref[...])
    lam = lam_ref[...]
    softplus_neg_lam = jnp.maximum(-lam, 0.0) + jnp.log1p(jnp.exp(-jnp.abs(lam)))
    log_a = -LRU_C * rec * softplus_neg_lam
    a_s[...] = jnp.exp(log_a)
    y2 = 2.0 * log_a
    e2 = jnp.exp(y2)
    em1 = e2 - 1.0
    expm1 = jnp.where(em1 == 0.0, y2, jnp.where(e2 == 0.0, -1.0, em1 * y2 / jnp.log(e2)))
    u_s[...] = jnp.sqrt(-expm1) * (inp * xc)

    def step(t, h):
        tt = jnp.where(d == 0, t, L - 1 - t)
        h = a_s[pl.ds(tt, 1), :] * h + u_s[pl.ds(tt, 1), :]
        o_ref[pl.ds(tt, 1), :] = h
        return h

    carry[...] = lax.fori_loop(0, L, step, carry[...], unroll=8)


def _lru(rest, conv_w, conv_b, wa, ba, wx, bx, lam, L):
    B, S, _ = rest.shape
    C = conv_w.shape[1]
    nblk = C // LRU_BLOCK
    L = min(L, S)
    nch = S // L
    hb = L // LRU_HALO

    def chunk(d, i):
        return jnp.where(d == 0, i, nch - 1 - i)

    vec = lambda: pl.BlockSpec((None, 1, C), lambda d, b, i: (d, 0, 0))
    mat = lambda: pl.BlockSpec((None, nblk, LRU_BLOCK, LRU_BLOCK), lambda d, b, i: (d, 0, 0, 0))
    return pl.pallas_call(
        functools.partial(_lru_kernel, L=L, nch=nch, nblk=nblk),
        grid=(2, B, nch),
        in_specs=[
            pl.BlockSpec((None, LRU_HALO, C), lambda d, b, i: (b, jnp.maximum(chunk(d, i) * hb - 1, 0), 0)),
            pl.BlockSpec((None, L, C), lambda d, b, i: (b, chunk(d, i), 0)),
            pl.BlockSpec((None, LRU_HALO, C),
                         lambda d, b, i: (b, jnp.minimum((chunk(d, i) + 1) * hb, S // LRU_HALO - 1), 0)),
            pl.BlockSpec((4, C), lambda d, b, i: (0, 0)),
            pl.BlockSpec((1, C), lambda d, b, i: (0, 0)),
            mat(), vec(), mat(), vec(), vec(),
        ],
        out_specs=pl.BlockSpec((None, None, L, C), lambda d, b, i: (d, b, chunk(d, i), 0)),
        out_shape=jax.ShapeDtypeStruct((2, B, S, C), F32),
        scratch_shapes=[pltpu.VMEM((L + 2 * LRU_HALO, C), F32), pltpu.VMEM((L, C), F32),
                        pltpu.VMEM((L, C), F32), pltpu.VMEM((1, C), F32)],
        compiler_params=_params("arbitrary", "arbitrary", "arbitrary"),
        name="rg_lru",
    )(rest, rest, rest, conv_w, conv_b.reshape(1, C), wa, ba.reshape(2, 1, C), wx, bx.reshape(2, 1, C),
      lam.reshape(2, 1, C))


def _merge_kernel(ya_ref, hf_ref, hb_ref, gr_ref, ga_ref, gb_ref, wa_ref, wr_ref, o_ref):
    yr = (hf_ref[...] + hb_ref[...]) * _gelu(gr_ref[...])
    pa = jnp.dot(ya_ref[...], wa_ref[...], preferred_element_type=F32)
    pr = jnp.dot(yr.astype(BF16), wr_ref[...], preferred_element_type=F32)
    o_ref[...] = (_sigmoid(ga_ref[...]) * pa + _sigmoid(gb_ref[...]) * pr).astype(o_ref.dtype)


def _merge(ya, hs, rest, wba, wbr, tm):
    T, C = ya.shape
    D = wba.shape[1]
    tm = min(tm, T)
    assert rest.shape[1] == 2 * C + 2 * D and D % C == 0
    return pl.pallas_call(
        _merge_kernel,
        grid=(T // tm,),
        in_specs=[
            pl.BlockSpec((tm, C), lambda i: (i, 0)),
            pl.BlockSpec((None, tm, C), lambda i: (0, i, 0)),
            pl.BlockSpec((None, tm, C), lambda i: (1, i, 0)),
            pl.BlockSpec((tm, C), lambda i: (i, 1)),
            pl.BlockSpec((tm, D), lambda i: (i, (2 * C) // D)),
            pl.BlockSpec((tm, D), lambda i: (i, (2 * C) // D + 1)),
            pl.BlockSpec((C, D), lambda i: (0, 0)),
            pl.BlockSpec((C, D), lambda i: (0, 0)),
        ],
        out_specs=pl.BlockSpec((tm, D), lambda i: (i, 0)),
        out_shape=jax.ShapeDtypeStruct((T, D), BF16),
        compiler_params=_params("parallel"),
        name="branch_merge",
    )(ya, hs, hs, rest, rest, rest, wba, wbr)


def _topk_rows(s, iota, n):
    vals, idxs = [], []
    for _ in range(PEER_TOPK):
        m = jnp.max(s, axis=0, keepdims=True)
        idx = jnp.min(jnp.where(s == m, iota, n), axis=0, keepdims=True)
        vals.append(m)
        idxs.append(idx)
        s = jnp.where(iota == idx, -jnp.inf, s)
    return jnp.concatenate(vals, axis=0), jnp.concatenate(idxs, axis=0)


def _route_kernel(h_ref, gn_ref, wq_ref, sk_ref, xn_ref, ids_ref, g_ref, q_s, ids_s, g_s, *, heads):
    tm = h_ref.shape[0]
    xn = _rms(h_ref[...], gn_ref[...])
    xn_ref[...] = xn
    q = jnp.dot(xn.astype(BF16), wq_ref[...], preferred_element_type=F32).astype(BF16)
    for hp in range(2 * heads):
        q_s[hp] = q[:, hp * PEER_NKEYS:(hp + 1) * PEER_NKEYS]
    K = PEER_TOPK
    iota_keys = lax.broadcasted_iota(I32, (PEER_NKEYS, tm), 0)
    iota_k = lax.broadcasted_iota(I32, (K, tm), 0)
    ka = 4
    kb = K // (ka + 1)
    cand_idx = jnp.concatenate([k1 * K + iota_k for k1 in range(ka)] + [iota_k * K + k2 for k2 in range(kb)], axis=0)

    def head_body(h, carry):
        sv, si = [], []
        for p in range(2):
            s = lax.dot_general(sk_ref[2 * h + p], q_s[2 * h + p], (((1,), (1,)), ((), ())),
                                preferred_element_type=F32)
            v, ix = _topk_rows(s, iota_keys, PEER_NKEYS)
            sv.append(v)
            si.append(ix)
        cand = jnp.concatenate(
            [sv[0][k1:k1 + 1, :] + sv[1] for k1 in range(ka)]
            + [jnp.where(iota_k >= ka, sv[0] + sv[1][k2:k2 + 1, :], -jnp.inf) for k2 in range(kb)], axis=0)
        cv, ci = _topk_rows(cand, cand_idx, K * K)
        ids = []
        for k in range(K):
            c = ci[k:k + 1, :]
            i1 = jnp.sum(jnp.where(iota_k == (c >> PEER_TOPK_LOG2), si[0], 0), axis=0, keepdims=True)
            i2 = jnp.sum(jnp.where(iota_k == (c & (K - 1)), si[1], 0), axis=0, keepdims=True)
            ids.append(i1 * PEER_NKEYS + i2)
        e = jnp.exp(cv - cv[0:1, :])
        row = pl.multiple_of(h * K, K)
        ids_s[pl.ds(row, K), :] = jnp.concatenate(ids, axis=0).astype(F32)
        g_s[pl.ds(row, K), :] = e / jnp.sum(e, axis=0, keepdims=True)
        return carry

    lax.fori_loop(0, heads, head_body, 0)
    ids_ref[...] = ids_s[...].T.astype(I32)
    g_ref[...] = g_s[...].T


def _route(h, gn, wq, sk, tm):
    T, D = h.shape
    NQ = wq.shape[1]
    heads = NQ // (2 * PEER_NKEYS)
    nsel = heads * PEER_TOPK
    tm = min(tm, T)
    return pl.pallas_call(
        functools.partial(_route_kernel, heads=heads),
        grid=(T // tm,),
        in_specs=[
            pl.BlockSpec((tm, D), lambda i: (i, 0)),
            pl.BlockSpec((1, D), lambda i: (0, 0)),
            pl.BlockSpec((D, NQ), lambda i: (0, 0)),
            pl.BlockSpec(sk.shape, lambda i: (0, 0, 0)),
        ],
        out_specs=[
            pl.BlockSpec((tm, D), lambda i: (i, 0)),
            pl.BlockSpec((tm, nsel), lambda i: (i, 0)),
            pl.BlockSpec((tm, nsel), lambda i: (i, 0)),
        ],
        out_shape=[
            jax.ShapeDtypeStruct((T, D), F32),
            jax.ShapeDtypeStruct((T, nsel), I32),
            jax.ShapeDtypeStruct((T, nsel), F32),
        ],
        scratch_shapes=[pltpu.VMEM((2 * heads, tm, PEER_NKEYS), BF16), pltpu.VMEM((nsel, tm), F32),
                        pltpu.VMEM((nsel, tm), F32)],
        compiler_params=_params("parallel"),
        name="peer_route",
    )(h, gn.reshape(1, D), wq, sk)


def _pack_bf16_pairs(lo, hi):
    lo16 = lax.bitcast_convert_type(lo.astype(BF16), jnp.uint16).astype(U32)
    hi16 = lax.bitcast_convert_type(hi.astype(BF16), jnp.uint16).astype(U32)
    return lo16 | (hi16 << 16)


def _unpack_lo(words):
    return lax.bitcast_convert_type(words << 16, F32)


def _unpack_hi(words):
    return lax.bitcast_convert_type(words & np.uint32(0xFFFF0000), F32)


def _gather_kernel(ids_hbm, x_ref, g_ref, h_ref, tab_hbm, o_ref, ids_s, sem, isem, *bufs, tb, nbuf, nsel):
    i = pl.program_id(0)
    D = x_ref.shape[1]
    nlt = D // LANES
    nsteps = pl.num_programs(0)
    win = (tb + nbuf) * nsel
    half = lax.rem(i, 2)

    def ids_copy(step, h):
        return pltpu.make_async_copy(ids_hbm.at[pl.ds(pl.multiple_of(step * (tb * nsel), 1024), win)],
                                     ids_s.at[pl.ds(pl.multiple_of(h * win, 1024), win)], isem.at[h])

    @pl.when(i == 0)
    def _():
        ids_copy(0, 0).start()

    ids_copy(i, half).wait()

    @pl.when(i + 1 < nsteps)
    def _():
        ids_copy(i + 1, 1 - half).start()

    ids_base = half * win

    def issue(t, slot, n0, n1):
        base = ids_base + t * nsel
        for n in range(n0, n1):
            e = ids_s[base + n]
            pltpu.make_async_copy(tab_hbm.at[e], bufs[slot].at[pl.ds(n, 1), :], sem.at[slot]).start(priority=n % 2)

    def wait(slot):
        pltpu.make_async_copy(bufs[slot], bufs[slot], sem.at[slot]).wait()

    @pl.when(i == 0)
    def _():
        for s in range(nbuf - 1):
            issue(s, s, 0, nsel)

    eye = lax.broadcasted_iota(I32, (nsel, nsel), 0) == lax.broadcasted_iota(I32, (nsel, nsel), 1)
    per_chunk = nsel // (2 * nlt)
    assert per_chunk * 2 * nlt == nsel

    def group(gi, carry):
        for b in range(nbuf):
            t = gi * nbuf + b
            nt = t + (nbuf - 1)
            nslot = (b + nbuf - 1) % nbuf
            wait(b)
            xb = x_ref[pl.ds(t, 1), :]
            acc = None
            for j in range(nlt):
                cs = slice(j * LANES, (j + 1) * LANES)
                term = _unpack_lo(bufs[b][:, cs]) * xb[:, cs]
                acc = term if acc is None else acc + term
                issue(nt, nslot, j * per_chunk, (j + 1) * per_chunk)
            score = jnp.sum(acc, axis=1, keepdims=True)
            gcol = jnp.sum(jnp.where(eye, g_ref[pl.ds(t, 1), :], 0.0), axis=1, keepdims=True)
            w = gcol * _gelu(score)
            ys = []
            for j in range(nlt):
                ys.append(jnp.sum(_unpack_hi(bufs[b][:, j * LANES:(j + 1) * LANES]) * w, axis=0, keepdims=True))
                issue(nt, nslot, (nlt + j) * per_chunk, (nlt + j + 1) * per_chunk)
            o_ref[pl.ds(t, 1), :] = h_ref[pl.ds(t, 1), :] + jnp.concatenate(ys, axis=1)
        return carry

    lax.fori_loop(0, tb // nbuf, group, 0)

    @pl.when(i == nsteps - 1)
    def _():
        for s in range(nbuf - 1):
            wait(s)


def _gather(ids, xn, g, h, tab, tb, nbuf):
    T, D = xn.shape
    nsel = g.shape[1]
    tb = min(tb, T)
    assert tb % nbuf == 0 and T % tb == 0
    ids = jnp.concatenate([ids, jnp.zeros((nbuf * nsel,), I32)])
    return pl.pallas_call(
        functools.partial(_gather_kernel, tb=tb, nbuf=nbuf, nsel=nsel),
        grid=(T // tb,),
        in_specs=[
            pl.BlockSpec(memory_space=pl.ANY),
            pl.BlockSpec((tb, D), lambda i: (i, 0)),
            pl.BlockSpec((tb, nsel), lambda i: (i, 0)),
            pl.BlockSpec((tb, D), lambda i: (i, 0)),
            pl.BlockSpec(memory_space=pl.ANY),
        ],
        out_specs=pl.BlockSpec((tb, D), lambda i: (i, 0)),
        out_shape=jax.ShapeDtypeStruct((T, D), F32),
        scratch_shapes=[pltpu.SMEM((2 * (tb + nbuf) * nsel,), I32), pltpu.SemaphoreType.DMA((nbuf,)),
                        pltpu.SemaphoreType.DMA((2,))] + [pltpu.VMEM((nsel, D), U32) for _ in range(nbuf)],
        compiler_params=_params("arbitrary"),
        name="peer_gather",
    )(ids, xn, g, h, tab.reshape(tab.shape[0], 1, D))


def _ple_kernel(h_ref, p_ref, gn_ref, wg_ref, wp_ref, gf_ref, o_ref):
    h = h_ref[...]
    hn = _rms(h, gn_ref[...]).astype(BF16)
    gate = _sigmoid(jnp.dot(hn, wg_ref[...], preferred_element_type=F32))
    pp = jnp.dot(p_ref[...].astype(BF16), wp_ref[...], preferred_element_type=F32)
    o_ref[...] = _rms(h + gate * pp, gf_ref[...])


def _ple(h, p, gn, wg, wp, gf, tm):
    T, D = h.shape
    P = p.shape[1]
    tm = min(tm, T)
    return pl.pallas_call(
        _ple_kernel,
        grid=(T // tm,),
        in_specs=[
            pl.BlockSpec((tm, D), lambda i: (i, 0)),
            pl.BlockSpec((tm, P), lambda i: (i, 0)),
            pl.BlockSpec((1, D), lambda i: (0, 0)),
            pl.BlockSpec((D, D), lambda i: (0, 0)),
            pl.BlockSpec((P, D), lambda i: (0, 0)),
            pl.BlockSpec((1, D), lambda i: (0, 0)),
        ],
        out_specs=pl.BlockSpec((tm, D), lambda i: (i, 0)),
        out_shape=jax.ShapeDtypeStruct((T, D), F32),
        compiler_params=_params("parallel"),
        name="ple_final",
    )(h, p, gn.reshape(1, D), wg, wp, gf.reshape(1, D))


def _trunk(x, p, w):
    B, S, D = x.shape
    T = B * S
    x2 = x.reshape(T, D)
    qkv = _norm_matmul(x2, w["norm_mix"], w["w_in_qkv"], BF16, 1024, 1024)
    rest = _norm_matmul(x2, w["norm_mix"], w["w_in_rest"], F32, 1024, 1024)
    ya = _na(qkv.reshape(B, S, -1), w["na_bias"])
    hs = _lru(rest.reshape(B, S, -1), w["conv_w"], w["conv_b"], w["lru_wa"], w["lru_ba"], w["lru_wx"],
              w["lru_bx"], w["lru_lambda"], 512)
    C = hs.shape[-1]
    merged = _merge(ya.reshape(T, -1), hs.reshape(2, T, C), rest, w["w_branch_a"], w["w_branch_r"], 256)
    h1 = _matmul_res(merged, w["w_out"], x2, 512, 1024)
    xn, ids, g = _route(h1, w["norm_ffn"], w["peer_wq"], w["peer_subkeys"], 256)
    h2 = _gather(ids.reshape(-1), xn, g, h1, w["peer_tab"], 128, 8)
    out = _ple(h2, p.reshape(T, -1), w["norm_ple"], w["ple_gate_w"], w["ple_proj_w"], w["final_norm"], 256)
    return out.reshape(B, S, D)


def kernel(x_prompt, x_sample, p_prompt, p_sample, norm_mix, w_in, na_rpb, conv_w, conv_b, lru_wa, lru_ba, lru_wx, lru_bx, lru_lambda, w_branch_a, w_branch_r, w_out, norm_ffn, peer_wq, peer_subkeys, peer_u, peer_v, norm_ple, ple_gate_w, ple_proj_w, final_norm):
    assert w_in.shape[0] == 1, "single layer"
    na_w = w_branch_a.shape[1]
    sk = peer_subkeys[0]
    w = dict(
        norm_mix=norm_mix[0],
        w_in_qkv=w_in[0][:, :3 * na_w].astype(BF16),
        w_in_rest=w_in[0][:, 3 * na_w:].astype(BF16),
        na_bias=_na_bias_table(na_rpb[0]),
        conv_w=conv_w[0], conv_b=conv_b[0],
        lru_wa=lru_wa[0].astype(BF16), lru_ba=lru_ba[0], lru_wx=lru_wx[0].astype(BF16), lru_bx=lru_bx[0],
        lru_lambda=lru_lambda[0],
        w_branch_a=w_branch_a[0].astype(BF16), w_branch_r=w_branch_r[0].astype(BF16),
        w_out=w_out[0].astype(BF16),
        norm_ffn=norm_ffn[0], peer_wq=peer_wq[0].astype(BF16),
        peer_subkeys=sk.reshape(sk.shape[0] * 2, sk.shape[2], sk.shape[3]).astype(BF16),
        peer_tab=_pack_bf16_pairs(peer_u[0], peer_v[0]),
        norm_ple=norm_ple[0], ple_gate_w=ple_gate_w[0].astype(BF16), ple_proj_w=ple_proj_w[0].astype(BF16),
        final_norm=final_norm,
    )
    return (_trunk(x_prompt, p_prompt[0], w), _trunk(x_sample, p_sample[0], w))
```

```python
import functools
import math

import numpy as np
import jax
import jax.numpy as jnp
from jax import lax
from jax.experimental import pallas as pl
from jax.experimental.pallas import tpu as pltpu

F32 = jnp.float32
BF16 = jnp.bfloat16
I32 = jnp.int32
U32 = jnp.uint32

LANES = 128
GRID_W = 64
NA_HEAD_DIM = 128
NA_KH = 8
NA_KW = 16
NA_ROWS_PER_STEP = 8
NEG_INF = -1e30
LRU_BLOCK = 128
LRU_C = 8.0
LRU_HALO = 8
PEER_NKEYS = 128
PEER_TOPK = 16
PEER_TOPK_LOG2 = 4
assert 1 << PEER_TOPK_LOG2 == PEER_TOPK
RMS_EPS = 1e-6
VMEM_LIMIT = 50 * 1024 * 1024


def _rms(x, g):
    return x * lax.rsqrt(jnp.mean(x * x, axis=-1, keepdims=True) + RMS_EPS) * g


def _gelu(x):
    return 0.5 * x * (1.0 + lax.erf(x * np.float32(math.sqrt(0.5))))


def _sigmoid(x):
    return jax.nn.sigmoid(x)


def _params(*sem):
    return pltpu.CompilerParams(dimension_semantics=sem, vmem_limit_bytes=VMEM_LIMIT)


def _norm_matmul_kernel(x_ref, g_ref, w_ref, o_ref, xn_ref):
    @pl.when(pl.program_id(1) == 0)
    def _():
        xn_ref[...] = _rms(x_ref[...], g_ref[...]).astype(BF16)

    o_ref[...] = jnp.dot(xn_ref[...], w_ref[...], preferred_element_type=F32).astype(o_ref.dtype)


def _norm_matmul(x, g, w, out_dtype, tm, tn):
    T, D = x.shape
    N = w.shape[1]
    tm, tn = min(tm, T), min(tn, N)
    return pl.pallas_call(
        _norm_matmul_kernel,
        grid=(T // tm, N // tn),
        in_specs=[
            pl.BlockSpec((tm, D), lambda i, j: (i, 0)),
            pl.BlockSpec((1, D), lambda i, j: (0, 0)),
            pl.BlockSpec((D, tn), lambda i, j: (0, j)),
        ],
        out_specs=pl.BlockSpec((tm, tn), lambda i, j: (i, j)),
        out_shape=jax.ShapeDtypeStruct((T, N), out_dtype),
        scratch_shapes=[pltpu.VMEM((tm, D), BF16)],
        compiler_params=_params("parallel", "arbitrary"),
        name="norm_matmul",
    )(x, g.reshape(1, D), w)


def _matmul_res_kernel(a_ref, w_ref, r_ref, o_ref):
    o_ref[...] = r_ref[...] + jnp.dot(a_ref[...], w_ref[...], preferred_element_type=F32)


def _matmul_res(a, w, res, tm, tn):
    T, K = a.shape
    N = w.shape[1]
    tm, tn = min(tm, T), min(tn, N)
    return pl.pallas_call(
        _matmul_res_kernel,
        grid=(T // tm, N // tn),
        in_specs=[
            pl.BlockSpec((tm, K), lambda i, j: (i, 0)),
            pl.BlockSpec((K, tn), lambda i, j: (0, j)),
            pl.BlockSpec((tm, tn), lambda i, j: (i, j)),
        ],
        out_specs=pl.BlockSpec((tm, tn), lambda i, j: (i, j)),
        out_shape=jax.ShapeDtypeStruct((T, N), F32),
        compiler_params=_params("parallel", "arbitrary"),
        name="matmul_res",
    )(a, w, res)


def _na_bias_table(rpb):
    qc = np.arange(GRID_W)
    kc = np.arange(GRID_W)
    col_start = np.clip(qc - NA_KW // 2, 0, GRID_W - NA_KW)
    ok = (kc[None, :] >= col_start[:, None]) & (kc[None, :] < col_start[:, None] + NA_KW)
    dc = kc[None, :] - qc[:, None] + (NA_KW - 1)
    onehot = ((dc[None] == np.arange(2 * NA_KW - 1)[:, None, None]) & ok[None]).astype(np.float32)
    t = jnp.einsum("hrd,dqk->hrqk", rpb.astype(F32), onehot, precision=lax.Precision.HIGHEST)
    t = jnp.where(ok[None, None], t, NEG_INF)
    per_off = [jnp.concatenate([t[:, slot - off + NA_KH - 1] for slot in range(NA_KH)], axis=-1)
               for off in range(NA_KH)]
    return jnp.stack(per_off, axis=1)


def _na_kernel(q_ref, kp_ref, kc_ref, kn_ref, vp_ref, vc_ref, vn_ref, b_ref, o_ref, kbuf, vbuf,
               *, rows, heads):
    i = pl.program_id(1)
    tb = NA_ROWS_PER_STEP * GRID_W
    nkeys = NA_KH * GRID_W
    halo = (NA_KH // 2) * GRID_W
    kbuf[0:halo] = kp_ref[...]
    kbuf[halo:halo + tb] = kc_ref[...]
    kbuf[halo + tb:2 * halo + tb] = kn_ref[...]
    vbuf[0:halo] = vp_ref[...]
    vbuf[halo:halo + tb] = vc_ref[...]
    vbuf[halo + tb:2 * halo + tb] = vn_ref[...]
    scale = np.float32(NA_HEAD_DIM ** -0.5)

    def row_body(rr, carry):
        r = i * NA_ROWS_PER_STEP + rr
        rs = jnp.clip(r - NA_KH // 2, 0, rows - NA_KH)
        woff = r - rs
        koff = pl.multiple_of((rs - (i * NA_ROWS_PER_STEP - NA_KH // 2)) * GRID_W, GRID_W)
        qoff = pl.multiple_of(rr * GRID_W, GRID_W)
        cols = [slice(h * NA_HEAD_DIM, (h + 1) * NA_HEAD_DIM) for h in range(heads)]
        scores = [lax.dot_general(q_ref[pl.ds(qoff, GRID_W), cs], kbuf[pl.ds(koff, nkeys), cs],
                                  (((1,), (1,)), ((), ())), preferred_element_type=F32) for cs in cols]
        probs, norms = [], []
        for h in range(heads):
            s = scores[h] * scale + b_ref[h, woff]
            p = jnp.exp(s - jnp.max(s, axis=-1, keepdims=True))
            norms.append(jnp.sum(p, axis=-1, keepdims=True))
            probs.append(p.astype(BF16))
        outs = [jnp.dot(probs[h], vbuf[pl.ds(koff, nkeys), cols[h]], preferred_element_type=F32) / norms[h]
                for h in range(heads)]
        o_ref[pl.ds(qoff, GRID_W), :] = jnp.concatenate(outs, axis=1).astype(o_ref.dtype)
        return carry

    lax.fori_loop(0, NA_ROWS_PER_STEP, row_body, 0)


def _na(qkv, bias):
    B, S, W3 = qkv.shape
    W = W3 // 3
    heads = W // NA_HEAD_DIM
    rows = S // GRID_W
    assert rows % NA_ROWS_PER_STEP == 0 and rows >= NA_KH
    nblk = rows // NA_ROWS_PER_STEP
    tb = NA_ROWS_PER_STEP * GRID_W

    halo = (NA_KH // 2) * GRID_W
    hpb = tb // halo
    assert hpb * halo == tb

    def spec(col):
        return pl.BlockSpec((None, tb, W), lambda b, i: (b, i, col))

    def above(col):
        return pl.BlockSpec((None, halo, W), lambda b, i: (b, jnp.maximum(i * hpb - 1, 0), col))

    def below(col):
        return pl.BlockSpec((None, halo, W), lambda b, i: (b, jnp.minimum((i + 1) * hpb, nblk * hpb - 1), col))

    return pl.pallas_call(
        functools.partial(_na_kernel, rows=rows, heads=heads),
        grid=(B, nblk),
        in_specs=[spec(0), above(1), spec(1), below(1), above(2), spec(2), below(2),
                  pl.BlockSpec(bias.shape, lambda b, i: (0, 0, 0, 0))],
        out_specs=pl.BlockSpec((None, tb, W), lambda b, i: (b, i, 0)),
        out_shape=jax.ShapeDtypeStruct((B, S, W), BF16),
        scratch_shapes=[pltpu.VMEM((tb + 2 * halo, W), BF16), pltpu.VMEM((tb + 2 * halo, W), BF16)],
        compiler_params=_params("parallel", "arbitrary"),
        name="na_attention",
    )(qkv, qkv, qkv, qkv, qkv, qkv, qkv, bias)


def _lru_kernel(xp_ref, x_ref, xn_ref, cw_ref, cb_ref, wa_ref, ba_ref, wx_ref, bx_ref, lam_ref,
                o_ref, xe, a_s, u_s, carry, *, L, nch, nblk):
    d = pl.program_id(0)
    i = pl.program_id(2)
    c = jnp.where(d == 0, i, nch - 1 - i)

    @pl.when(i == 0)
    def _():
        carry[...] = jnp.zeros_like(carry)

    xe[0:LRU_HALO] = jnp.where(c > 0, xp_ref[...], 0.0)
    xe[LRU_HALO:LRU_HALO + L] = x_ref[...]
    xe[LRU_HALO + L:2 * LRU_HALO + L] = jnp.where(c < nch - 1, xn_ref[...], 0.0)
    cw = cw_ref[...]
    xc = cb_ref[...] + cw[2:3] * xe[pl.ds(LRU_HALO, L), :]
    xc = xc + cw[0:1] * xe[pl.ds(LRU_HALO - 2, L), :]
    xc = xc + cw[1:2] * xe[pl.ds(LRU_HALO - 1, L), :]
    xc = xc + cw[3:4] * xe[pl.ds(LRU_HALO + 1, L), :]
    xcb = xc.astype(BF16)

    def blockdiag(w_ref):
        outs = []
        for n in range(nblk):
            cs = slice(n * LRU_BLOCK, (n + 1) * LRU_BLOCK)
            outs.append(jnp.dot(xcb[:, cs], w_ref[n], preferred_element_type=F32))
        return jnp.concatenate(outs, axis=1)

    rec = _sigmoid(blockdiag(wa_ref) + ba_ref[...])
    inp = _sigmoid(blockdiag(wx_ref) + bx_ref[...])
    lam = lam_ref[...]
    softplus_neg_lam = jnp.maximum(-lam, 0.0) + jnp.log1p(jnp.exp(-jnp.abs(lam)))
    log_a = -LRU_C * rec * softplus_neg_lam
    a_s[...] = jnp.exp(log_a)
    y2 = 2.0 * log_a
    e2 = jnp.exp(y2)
    em1 = e2 - 1.0
    expm1 = jnp.where(em1 == 0.0, y2, jnp.where(e2 == 0.0, -1.0, em1 * y2 / jnp.log(e2)))
    u_s[...] = jnp.sqrt(-expm1) * (inp * xc)

    def step(t, h):
        tt = jnp.where(d == 0, t, L - 1 - t)
        h = a_s[pl.ds(tt, 1), :] * h + u_s[pl.ds(tt, 1), :]
        o_ref[pl.ds(tt, 1), :] = h
        return h

    carry[...] = lax.fori_loop(0, L, step, carry[...], unroll=8)


def _lru(rest, conv_w, conv_b, wa, ba, wx, bx, lam, L):
    B, S, _ = rest.shape
    C = conv_w.shape[1]
    nblk = C // LRU_BLOCK
    L = min(L, S)
    nch = S // L
    hb = L // LRU_HALO

    def chunk(d, i):
        return jnp.where(d == 0, i, nch - 1 - i)

    vec = lambda: pl.BlockSpec((None, 1, C), lambda d, b, i: (d, 0, 0))
    mat = lambda: pl.BlockSpec((None, nblk, LRU_BLOCK, LRU_BLOCK), lambda d, b, i: (d, 0, 0, 0))
    return pl.pallas_call(
        functools.partial(_lru_kernel, L=L, nch=nch, nblk=nblk),
        grid=(2, B, nch),
        in_specs=[
            pl.BlockSpec((None, LRU_HALO, C), lambda d, b, i: (b, jnp.maximum(chunk(d, i) * hb - 1, 0), 0)),
            pl.BlockSpec((None, L, C), lambda d, b, i: (b, chunk(d, i), 0)),
            pl.BlockSpec((None, LRU_HALO, C),
                         lambda d, b, i: (b, jnp.minimum((chunk(d, i) + 1) * hb, S // LRU_HALO - 1), 0)),
            pl.BlockSpec((4, C), lambda d, b, i: (0, 0)),
            pl.BlockSpec((1, C), lambda d, b, i: (0, 0)),
            mat(), vec(), mat(), vec(), vec(),
        ],
        out_specs=pl.BlockSpec((None, None, L, C), lambda d, b, i: (d, b, chunk(d, i), 0)),
        out_shape=jax.ShapeDtypeStruct((2, B, S, C), F32),
        scratch_shapes=[pltpu.VMEM((L + 2 * LRU_HALO, C), F32), pltpu.VMEM((L, C), F32),
                        pltpu.VMEM((L, C), F32), pltpu.VMEM((1, C), F32)],
        compiler_params=_params("arbitrary", "arbitrary", "arbitrary"),
        name="rg_lru",
    )(rest, rest, rest, conv_w, conv_b.reshape(1, C), wa, ba.reshape(2, 1, C), wx, bx.reshape(2, 1, C),
      lam.reshape(2, 1, C))


def _merge_kernel(ya_ref, hf_ref, hb_ref, gr_ref, ga_ref, gb_ref, wa_ref, wr_ref, o_ref):
    yr = (hf_ref[...] + hb_ref[...]) * _gelu(gr_ref[...])
    pa = jnp.dot(ya_ref[...], wa_ref[...], preferred_element_type=F32)
    pr = jnp.dot(yr.astype(BF16), wr_ref[...], preferred_element_type=F32)
    o_ref[...] = (_sigmoid(ga_ref[...]) * pa + _sigmoid(gb_ref[...]) * pr).astype(o_ref.dtype)


def _merge(ya, hs, rest, wba, wbr, tm):
    T, C = ya.shape
    D = wba.shape[1]
    tm = min(tm, T)
    assert rest.shape[1] == 2 * C + 2 * D and D % C == 0
    return pl.pallas_call(
        _merge_kernel,
        grid=(T // tm,),
        in_specs=[
            pl.BlockSpec((tm, C), lambda i: (i, 0)),
            pl.BlockSpec((None, tm, C), lambda i: (0, i, 0)),
            pl.BlockSpec((None, tm, C), lambda i: (1, i, 0)),
            pl.BlockSpec((tm, C), lambda i: (i, 1)),
            pl.BlockSpec((tm, D), lambda i: (i, (2 * C) // D)),
            pl.BlockSpec((tm, D), lambda i: (i, (2 * C) // D + 1)),
            pl.BlockSpec((C, D), lambda i: (0, 0)),
            pl.BlockSpec((C, D), lambda i: (0, 0)),
        ],
        out_specs=pl.BlockSpec((tm, D), lambda i: (i, 0)),
        out_shape=jax.ShapeDtypeStruct((T, D), BF16),
        compiler_params=_params("parallel"),
        name="branch_merge",
    )(ya, hs, hs, rest, rest, rest, wba, wbr)


def _topk_rows(s, iota, n):
    vals, idxs = [], []
    for _ in range(PEER_TOPK):
        m = jnp.max(s, axis=0, keepdims=True)
        idx = jnp.min(jnp.where(s == m, iota, n), axis=0, keepdims=True)
        vals.append(m)
        idxs.append(idx)
        s = jnp.where(iota == idx, -jnp.inf, s)
    return jnp.concatenate(vals, axis=0), jnp.concatenate(idxs, axis=0)


def _route_kernel(h_ref, gn_ref, wq_ref, sk_ref, xn_ref, ids_ref, g_ref, q_s, ids_s, g_s, *, heads):
    tm = h_ref.shape[0]
    xn = _rms(h_ref[...], gn_ref[...])
    xn_ref[...] = xn
    q = jnp.dot(xn.astype(BF16), wq_ref[...], preferred_element_type=F32).astype(BF16)
    for hp in range(2 * heads):
        q_s[hp] = q[:, hp * PEER_NKEYS:(hp + 1) * PEER_NKEYS]
    K = PEER_TOPK
    iota_keys = lax.broadcasted_iota(I32, (PEER_NKEYS, tm), 0)
    iota_k = lax.broadcasted_iota(I32, (K, tm), 0)
    ka = 4
    kb = K // (ka + 1)
    cand_idx = jnp.concatenate([k1 * K + iota_k for k1 in range(ka)] + [iota_k * K + k2 for k2 in range(kb)], axis=0)

    def head_body(h, carry):
        sv, si = [], []
        for p in range(2):
            s = lax.dot_general(sk_ref[2 * h + p], q_s[2 * h + p], (((1,), (1,)), ((), ())),
                                preferred_element_type=F32)
            v, ix = _topk_rows(s, iota_keys, PEER_NKEYS)
            sv.append(v)
            si.append(ix)
        cand = jnp.concatenate(
            [sv[0][k1:k1 + 1, :] + sv[1] for k1 in range(ka)]
            + [jnp.where(iota_k >= ka, sv[0] + sv[1][k2:k2 + 1, :], -jnp.inf) for k2 in range(kb)], axis=0)
        cv, ci = _topk_rows(cand, cand_idx, K * K)
        ids = []
        for k in range(K):
            c = ci[k:k + 1, :]
            i1 = jnp.sum(jnp.where(iota_k == (c >> PEER_TOPK_LOG2), si[0], 0), axis=0, keepdims=True)
            i2 = jnp.sum(jnp.where(iota_k == (c & (K - 1)), si[1], 0), axis=0, keepdims=True)
            ids.append(i1 * PEER_NKEYS + i2)
        e = jnp.exp(cv - cv[0:1, :])
        row = pl.multiple_of(h * K, K)
        ids_s[pl.ds(row, K), :] = jnp.concatenate(ids, axis=0).astype(F32)
        g_s[pl.ds(row, K), :] = e / jnp.sum(e, axis=0, keepdims=True)
        return carry

    lax.fori_loop(0, heads, head_body, 0)
    ids_ref[...] = ids_s[...].T.astype(I32)
    g_ref[...] = g_s[...].T


def _route(h, gn, wq, sk, tm):
    T, D = h.shape
    NQ = wq.shape[1]
    heads = NQ // (2 * PEER_NKEYS)
    nsel = heads * PEER_TOPK
    tm = min(tm, T)
    return pl.pallas_call(
        functools.partial(_route_kernel, heads=heads),
        grid=(T // tm,),
        in_specs=[
            pl.BlockSpec((tm, D), lambda i: (i, 0)),
            pl.BlockSpec((1, D), lambda i: (0, 0)),
            pl.BlockSpec((D, NQ), lambda i: (0, 0)),
            pl.BlockSpec(sk.shape, lambda i: (0, 0, 0)),
        ],
        out_specs=[
            pl.BlockSpec((tm, D), lambda i: (i, 0)),
            pl.BlockSpec((tm, nsel), lambda i: (i, 0)),
            pl.BlockSpec((tm, nsel), lambda i: (i, 0)),
        ],
        out_shape=[
            jax.ShapeDtypeStruct((T, D), F32),
            jax.ShapeDtypeStruct((T, nsel), I32),
            jax.ShapeDtypeStruct((T, nsel), F32),
        ],
        scratch_shapes=[pltpu.VMEM((2 * heads, tm, PEER_NKEYS), BF16), pltpu.VMEM((nsel, tm), F32),
                        pltpu.VMEM((nsel, tm), F32)],
        compiler_params=_params("parallel"),
        name="peer_route",
    )(h, gn.reshape(1, D), wq, sk)


def _pack_bf16_pairs(lo, hi):
    lo16 = lax.bitcast_convert_type(lo.astype(BF16), jnp.uint16).astype(U32)
    hi16 = lax.bitcast_convert_type(hi.astype(BF16), jnp.uint16).astype(U32)
    return lo16 | (hi16 << 16)


def _unpack_lo(words):
    return lax.bitcast_convert_type(words << 16, F32)


def _unpack_hi(words):
    return lax.bitcast_convert_type(words & np.uint32(0xFFFF0000), F32)


def _gather_kernel(ids_hbm, x_ref, g_ref, h_ref, tab_hbm, o_ref, ids_s, sem, isem, *bufs, tb, nbuf, nsel):
    i = pl.program_id(0)
    D = x_ref.shape[1]
    nlt = D // LANES
    nsteps = pl.num_programs(0)
    win = (tb + nbuf) * nsel
    half = lax.rem(i, 2)

    def ids_copy(step, h):
        return pltpu.make_async_copy(ids_hbm.at[pl.ds(pl.multiple_of(step * (tb * nsel), 1024), win)],
                                     ids_s.at[pl.ds(pl.multiple_of(h * win, 1024), win)], isem.at[h])

    @pl.when(i == 0)
    def _():
        ids_copy(0, 0).start()

    ids_copy(i, half).wait()

    @pl.when(i + 1 < nsteps)
    def _():
        ids_copy(i + 1, 1 - half).start()

    ids_base = half * win

    def issue(t, slot, n0, n1):
        base = ids_base + t * nsel
        for n in range(n0, n1):
            e = ids_s[base + n]
            pltpu.make_async_copy(tab_hbm.at[e], bufs[slot].at[pl.ds(n, 1), :], sem.at[slot]).start(priority=n % 2)

    def wait(slot):
        pltpu.make_async_copy(bufs[slot], bufs[slot], sem.at[slot]).wait()

    @pl.when(i == 0)
    def _():
        for s in range(nbuf - 1):
            issue(s, s, 0, nsel)

    eye = lax.broadcasted_iota(I32, (nsel, nsel), 0) == lax.broadcasted_iota(I32, (nsel, nsel), 1)
    per_chunk = nsel // (2 * nlt)
    assert per_chunk * 2 * nlt == nsel

    def group(gi, carry):
        for b in range(nbuf):
            t = gi * nbuf + b
            nt = t + (nbuf - 1)
            nslot = (b + nbuf - 1) % nbuf
            wait(b)
            xb = x_ref[pl.ds(t, 1), :]
            acc = None
            for j in range(nlt):
                cs = slice(j * LANES, (j + 1) * LANES)
                term = _unpack_lo(bufs[b][:, cs]) * xb[:, cs]
                acc = term if acc is None else acc + term
                issue(nt, nslot, j * per_chunk, (j + 1) * per_chunk)
            score = jnp.sum(acc, axis=1, keepdims=True)
            gcol = jnp.sum(jnp.where(eye, g_ref[pl.ds(t, 1), :], 0.0), axis=1, keepdims=True)
            w = gcol * _gelu(score)
            ys = []
            for j in range(nlt):
                ys.append(jnp.sum(_unpack_hi(bufs[b][:, j * LANES:(j + 1) * LANES]) * w, axis=0, keepdims=True))
                issue(nt, nslot, (nlt + j) * per_chunk, (nlt + j + 1) * per_chunk)
            o_ref[pl.ds(t, 1), :] = h_ref[pl.ds(t, 1), :] + jnp.concatenate(ys, axis=1)
        return carry

    lax.fori_loop(0, tb // nbuf, group, 0)

    @pl.when(i == nsteps - 1)
    def _():
        for s in range(nbuf - 1):
            wait(s)


def _gather(ids, xn, g, h, tab, tb, nbuf):
    T, D = xn.shape
    nsel = g.shape[1]
    tb = min(tb, T)
    assert tb % nbuf == 0 and T % tb == 0
    ids = jnp.concatenate([ids, jnp.zeros((nbuf * nsel,), I32)])
    return pl.pallas_call(
        functools.partial(_gather_kernel, tb=tb, nbuf=nbuf, nsel=nsel),
        grid=(T // tb,),
        in_specs=[
            pl.BlockSpec(memory_space=pl.ANY),
            pl.BlockSpec((tb, D), lambda i: (i, 0)),
            pl.BlockSpec((tb, nsel), lambda i: (i, 0)),
            pl.BlockSpec((tb, D), lambda i: (i, 0)),
            pl.BlockSpec(memory_space=pl.ANY),
        ],
        out_specs=pl.BlockSpec((tb, D), lambda i: (i, 0)),
        out_shape=jax.ShapeDtypeStruct((T, D), F32),
        scratch_shapes=[pltpu.SMEM((2 * (tb + nbuf) * nsel,), I32), pltpu.SemaphoreType.DMA((nbuf,)),
                        pltpu.SemaphoreType.DMA((2,))] + [pltpu.VMEM((nsel, D), U32) for _ in range(nbuf)],
        compiler_params=_params("arbitrary"),
        name="peer_gather",
    )(ids, xn, g, h, tab.reshape(tab.shape[0], 1, D))


def _ple_kernel(h_ref, p_ref, gn_ref, wg_ref, wp_ref, gf_ref, o_ref):
    h = h_ref[...]
    hn = _rms(h, gn_ref[...]).astype(BF16)
    gate = _sigmoid(jnp.dot(hn, wg_ref[...], preferred_element_type=F32))
    pp = jnp.dot(p_ref[...].astype(BF16), wp_ref[...], preferred_element_type=F32)
    o_ref[...] = _rms(h + gate * pp, gf_ref[...])


def _ple(h, p, gn, wg, wp, gf, tm):
    T, D = h.shape
    P = p.shape[1]
    tm = min(tm, T)
    return pl.pallas_call(
        _ple_kernel,
        grid=(T // tm,),
        in_specs=[
            pl.BlockSpec((tm, D), lambda i: (i, 0)),
            pl.BlockSpec((tm, P), lambda i: (i, 0)),
            pl.BlockSpec((1, D), lambda i: (0, 0)),
            pl.BlockSpec((D, D), lambda i: (0, 0)),
            pl.BlockSpec((P, D), lambda i: (0, 0)),
            pl.BlockSpec((1, D), lambda i: (0, 0)),
        ],
        out_specs=pl.BlockSpec((tm, D), lambda i: (i, 0)),
        out_shape=jax.ShapeDtypeStruct((T, D), F32),
        compiler_params=_params("parallel"),
        name="ple_final",
    )(h, p, gn.reshape(1, D), wg, wp, gf.reshape(1, D))


def _trunk(x, p, w):
    B, S, D = x.shape
    T = B * S
    x2 = x.reshape(T, D)
    qkv = _norm_matmul(x2, w["norm_mix"], w["w_in_qkv"], BF16, 1024, 1024)
    rest = _norm_matmul(x2, w["norm_mix"], w["w_in_rest"], F32, 1024, 1024)
    ya = _na(qkv.reshape(B, S, -1), w["na_bias"])
    hs = _lru(rest.reshape(B, S, -1), w["conv_w"], w["conv_b"], w["lru_wa"], w["lru_ba"], w["lru_wx"],
              w["lru_bx"], w["lru_lambda"], 512)
    C = hs.shape[-1]
    merged = _merge(ya.reshape(T, -1), hs.reshape(2, T, C), rest, w["w_branch_a"], w["w_branch_r"], 256)
    h1 = _matmul_res(merged, w["w_out"], x2, 512, 1024)
    xn, ids, g = _route(h1, w["norm_ffn"], w["peer_wq"], w["peer_subkeys"], 256)
    h2 = _gather(ids.reshape(-1), xn, g, h1, w["peer_tab"], 128, 8)
    out = _ple(h2, p.reshape(T, -1), w["norm_ple"], w["ple_gate_w"], w["ple_proj_w"], w["final_norm"], 256)
    return out.reshape(B, S, D)


def kernel(x_prompt, x_sample, p_prompt, p_sample, norm_mix, w_in, na_rpb, conv_w, conv_b, lru_wa, lru_ba, lru_wx, lru_bx, lru_lambda, w_branch_a, w_branch_r, w_out, norm_ffn, peer_wq, peer_subkeys, peer_u, peer_v, norm_ple, ple_gate_w, ple_proj_w, final_norm):
    assert w_in.shape[0] == 1, "single layer"
    na_w = w_branch_a.shape[1]
    sk = peer_subkeys[0]
    w = dict(
        norm_mix=norm_mix[0],
        w_in_qkv=w_in[0][:, :3 * na_w].astype(BF16),
        w_in_rest=w_in[0][:, 3 * na_w:].astype(BF16),
        na_bias=_na_bias_table(na_rpb[0]),
        conv_w=conv_w[0], conv_b=conv_b[0],
        lru_wa=lru_wa[0].astype(BF16), lru_ba=lru_ba[0], lru_wx=lru_wx[0].astype(BF16), lru_bx=lru_bx[0],
        lru_lambda=lru_lambda[0],
        w_branch_a=w_branch_a[0].astype(BF16), w_branch_r=w_branch_r[0].astype(BF16),
        w_out=w_out[0].astype(BF16),
        norm_ffn=norm_ffn[0], peer_wq=peer_wq[0].astype(BF16),
        peer_subkeys=sk.reshape(sk.shape[0] * 2, sk.shape[2], sk.shape[3]).astype(BF16),
        peer_tab=_pack_bf16_pairs(peer_u[0], peer_v[0]),
        norm_ple=norm_ple[0], ple_gate_w=ple_gate_w[0].astype(BF16), ple_proj_w=ple_proj_w[0].astype(BF16),
        final_norm=final_norm,
    )
    return (_trunk(x_prompt, p_prompt[0], w), _trunk(x_sample, p_sample[0], w))
```

```python
import functools
import math

import numpy as np
import jax
import jax.numpy as jnp
from jax import lax
from jax.experimental import pallas as pl
from jax.experimental.pallas import tpu as pltpu

F32 = jnp.float32
BF16 = jnp.bfloat16
I32 = jnp.int32
U32 = jnp.uint32

LANES = 128
GRID_W = 64
NA_HEAD_DIM = 128
NA_KH = 8
NA_KW = 16
NA_ROWS_PER_STEP = 8
NEG_INF = -1e30
LRU_BLOCK = 128
LRU_C = 8.0
LRU_HALO = 8
PEER_NKEYS = 128
PEER_TOPK = 16
PEER_TOPK_LOG2 = 4
assert 1 << PEER_TOPK_LOG2 == PEER_TOPK
RMS_EPS = 1e-6

VMEM_LIMIT = 50 * 1024 * 1024
IN_PROJ_TILE = (1024, 1024)
OUT_PROJ_TILE = (512, 1024)
ROW_TILE = 256
LRU_CHUNK = 512
GATHER_TOKENS_PER_STEP = 256
GATHER_SLOTS = 8


def _rms(x, g):
    return x * lax.rsqrt(jnp.mean(x * x, axis=-1, keepdims=True) + RMS_EPS) * g


def _gelu(x):
    return 0.5 * x * (1.0 + lax.erf(x * np.float32(math.sqrt(0.5))))


def _sigmoid(x):
    return jax.nn.sigmoid(x)


def _params(*sem):
    return pltpu.CompilerParams(dimension_semantics=sem, vmem_limit_bytes=VMEM_LIMIT)


def _norm_matmul_kernel(x_ref, g_ref, w_ref, o_ref, xn_ref):
    @pl.when(pl.program_id(1) == 0)
    def _():
        xn_ref[...] = _rms(x_ref[...], g_ref[...]).astype(BF16)

    o_ref[...] = jnp.dot(xn_ref[...], w_ref[...], preferred_element_type=F32).astype(o_ref.dtype)


def _norm_matmul(x, g, w, out_dtype, tm, tn):
    T, D = x.shape
    N = w.shape[1]
    tm, tn = min(tm, T), min(tn, N)
    return pl.pallas_call(
        _norm_matmul_kernel,
        grid=(T // tm, N // tn),
        in_specs=[
            pl.BlockSpec((tm, D), lambda i, j: (i, 0)),
            pl.BlockSpec((1, D), lambda i, j: (0, 0)),
            pl.BlockSpec((D, tn), lambda i, j: (0, j)),
        ],
        out_specs=pl.BlockSpec((tm, tn), lambda i, j: (i, j)),
        out_shape=jax.ShapeDtypeStruct((T, N), out_dtype),
        scratch_shapes=[pltpu.VMEM((tm, D), BF16)],
        compiler_params=_params("parallel", "arbitrary"),
        name="norm_matmul",
    )(x, g.reshape(1, D), w)


def _matmul_res_kernel(a_ref, w_ref, r_ref, o_ref):
    o_ref[...] = r_ref[...] + jnp.dot(a_ref[...], w_ref[...], preferred_element_type=F32)


def _matmul_res(a, w, res, tm, tn):
    T, K = a.shape
    N = w.shape[1]
    tm, tn = min(tm, T), min(tn, N)
    return pl.pallas_call(
        _matmul_res_kernel,
        grid=(T // tm, N // tn),
        in_specs=[
            pl.BlockSpec((tm, K), lambda i, j: (i, 0)),
            pl.BlockSpec((K, tn), lambda i, j: (0, j)),
            pl.BlockSpec((tm, tn), lambda i, j: (i, j)),
        ],
        out_specs=pl.BlockSpec((tm, tn), lambda i, j: (i, j)),
        out_shape=jax.ShapeDtypeStruct((T, N), F32),
        compiler_params=_params("parallel", "arbitrary"),
        name="matmul_res",
    )(a, w, res)


def _na_bias_table(rpb):
    qc = np.arange(GRID_W)
    kc = np.arange(GRID_W)
    col_start = np.clip(qc - NA_KW // 2, 0, GRID_W - NA_KW)
    ok = (kc[None, :] >= col_start[:, None]) & (kc[None, :] < col_start[:, None] + NA_KW)
    dc = kc[None, :] - qc[:, None] + (NA_KW - 1)
    onehot = ((dc[None] == np.arange(2 * NA_KW - 1)[:, None, None]) & ok[None]).astype(np.float32)
    t = jnp.einsum("hrd,dqk->hrqk", rpb.astype(F32), onehot, precision=lax.Precision.HIGHEST)
    t = jnp.where(ok[None, None], t, NEG_INF)
    per_off = [jnp.concatenate([t[:, slot - off + NA_KH - 1] for slot in range(NA_KH)], axis=-1)
               for off in range(NA_KH)]
    return jnp.stack(per_off, axis=1)


def _na_kernel(q_ref, kp_ref, kc_ref, kn_ref, vp_ref, vc_ref, vn_ref, b_ref, o_ref, kbuf, vbuf,
               *, rows, heads):
    i = pl.program_id(1)
    tb = NA_ROWS_PER_STEP * GRID_W
    nkeys = NA_KH * GRID_W
    halo = (NA_KH // 2) * GRID_W
    kbuf[0:halo] = kp_ref[...]
    kbuf[halo:halo + tb] = kc_ref[...]
    kbuf[halo + tb:2 * halo + tb] = kn_ref[...]
    vbuf[0:halo] = vp_ref[...]
    vbuf[halo:halo + tb] = vc_ref[...]
    vbuf[halo + tb:2 * halo + tb] = vn_ref[...]
    scale = np.float32(NA_HEAD_DIM ** -0.5)

    def row_body(rr, carry):
        r = i * NA_ROWS_PER_STEP + rr
        rs = jnp.clip(r - NA_KH // 2, 0, rows - NA_KH)
        woff = r - rs
        koff = pl.multiple_of((rs - (i * NA_ROWS_PER_STEP - NA_KH // 2)) * GRID_W, GRID_W)
        qoff = pl.multiple_of(rr * GRID_W, GRID_W)
        cols = [slice(h * NA_HEAD_DIM, (h + 1) * NA_HEAD_DIM) for h in range(heads)]
        scores = [lax.dot_general(q_ref[pl.ds(qoff, GRID_W), cs], kbuf[pl.ds(koff, nkeys), cs],
                                  (((1,), (1,)), ((), ())), preferred_element_type=F32) for cs in cols]
        probs, norms = [], []
        for h in range(heads):
            s = scores[h] * scale + b_ref[h, woff]
            p = jnp.exp(s - jnp.max(s, axis=-1, keepdims=True))
            norms.append(jnp.sum(p, axis=-1, keepdims=True))
            probs.append(p.astype(BF16))
        outs = [jnp.dot(probs[h], vbuf[pl.ds(koff, nkeys), cols[h]], preferred_element_type=F32) / norms[h]
                for h in range(heads)]
        o_ref[pl.ds(qoff, GRID_W), :] = jnp.concatenate(outs, axis=1).astype(o_ref.dtype)
        return carry

    lax.fori_loop(0, NA_ROWS_PER_STEP, row_body, 0)


def _na(qkv, bias):
    B, S, W3 = qkv.shape
    W = W3 // 3
    heads = W // NA_HEAD_DIM
    rows = S // GRID_W
    assert rows % NA_ROWS_PER_STEP == 0 and rows >= NA_KH
    nblk = rows // NA_ROWS_PER_STEP
    tb = NA_ROWS_PER_STEP * GRID_W

    halo = (NA_KH // 2) * GRID_W
    hpb = tb // halo
    assert hpb * halo == tb

    def spec(col):
        return pl.BlockSpec((None, tb, W), lambda b, i: (b, i, col))

    def above(col):
        return pl.BlockSpec((None, halo, W), lambda b, i: (b, jnp.maximum(i * hpb - 1, 0), col))

    def below(col):
        return pl.BlockSpec((None, halo, W), lambda b, i: (b, jnp.minimum((i + 1) * hpb, nblk * hpb - 1), col))

    return pl.pallas_call(
        functools.partial(_na_kernel, rows=rows, heads=heads),
        grid=(B, nblk),
        in_specs=[spec(0), above(1), spec(1), below(1), above(2), spec(2), below(2),
                  pl.BlockSpec(bias.shape, lambda b, i: (0, 0, 0, 0))],
        out_specs=pl.BlockSpec((None, tb, W), lambda b, i: (b, i, 0)),
        out_shape=jax.ShapeDtypeStruct((B, S, W), BF16),
        scratch_shapes=[pltpu.VMEM((tb + 2 * halo, W), BF16), pltpu.VMEM((tb + 2 * halo, W), BF16)],
        compiler_params=_params("parallel", "arbitrary"),
        name="na_attention",
    )(qkv, qkv, qkv, qkv, qkv, qkv, qkv, bias)


def _lru_kernel(xp_ref, x_ref, xn_ref, cw_ref, cb_ref, wa_ref, ba_ref, wx_ref, bx_ref, lam_ref,
                o_ref, xe, a_s, u_s, carry, *, L, nch, nblk):
    d = pl.program_id(0)
    i = pl.program_id(2)
    c = jnp.where(d == 0, i, nch - 1 - i)

    @pl.when(i == 0)
    def _():
        carry[...] = jnp.zeros_like(carry)

    xe[0:LRU_HALO] = jnp.where(c > 0, xp_ref[...], 0.0)
    xe[LRU_HALO:LRU_HALO + L] = x_ref[...]
    xe[LRU_HALO + L:2 * LRU_HALO + L] = jnp.where(c < nch - 1, xn_ref[...], 0.0)
    cw = cw_ref[...]
    xc = cb_ref[...] + cw[2:3] * xe[pl.ds(LRU_HALO, L), :]
    xc = xc + cw[0:1] * xe[pl.ds(LRU_HALO - 2, L), :]
    xc = xc + cw[1:2] * xe[pl.ds(LRU_HALO - 1, L), :]
    xc = xc + cw[3:4] * xe[pl.ds(LRU_HALO + 1, L), :]
    xcb = xc.astype(BF16)

    def blockdiag(w_ref):
        outs = []
        for n in range(nblk):
            cs = slice(n * LRU_BLOCK, (n + 1) * LRU_BLOCK)
            outs.append(jnp.dot(xcb[:, cs], w_ref[n], preferred_element_type=F32))
        return jnp.concatenate(outs, axis=1)

    rec = _sigmoid(blockdiag(wa_ref) + ba_ref[...])
    inp = _sigmoid(blockdiag(wx_ref) + bx_ref[...])
    lam = lam_ref[...]
    softplus_neg_lam = jnp.maximum(-lam, 0.0) + jnp.log1p(jnp.exp(-jnp.abs(lam)))
    log_a = -LRU_C * rec * softplus_neg_lam
    a_s[...] = jnp.exp(log_a)
    th = jnp.tanh(log_a)
    u_s[...] = jnp.sqrt(-2.0 * th / (1.0 - th)) * (inp * xc)

    def step(t, h):
        tt = jnp.where(d == 0, t, L - 1 - t)
        h = a_s[pl.ds(tt, 1), :] * h + u_s[pl.ds(tt, 1), :]
        o_ref[pl.ds(tt, 1), :] = h
        return h

    carry[...] = lax.fori_loop(0, L, step, carry[...], unroll=8)


def _lru(rest, conv_w, conv_b, wa, ba, wx, bx, lam, L):
    B, S, _ = rest.shape
    C = conv_w.shape[1]
    nblk = C // LRU_BLOCK
    L = min(L, S)
    nch = S // L
    hb = L // LRU_HALO

    def chunk(d, i):
        return jnp.where(d == 0, i, nch - 1 - i)

    vec = lambda: pl.BlockSpec((None, 1, C), lambda d, b, i: (d, 0, 0))
    mat = lambda: pl.BlockSpec((None, nblk, LRU_BLOCK, LRU_BLOCK), lambda d, b, i: (d, 0, 0, 0))
    return pl.pallas_call(
        functools.partial(_lru_kernel, L=L, nch=nch, nblk=nblk),
        grid=(2, B, nch),
        in_specs=[
            pl.BlockSpec((None, LRU_HALO, C), lambda d, b, i: (b, jnp.maximum(chunk(d, i) * hb - 1, 0), 0)),
            pl.BlockSpec((None, L, C), lambda d, b, i: (b, chunk(d, i), 0)),
            pl.BlockSpec((None, LRU_HALO, C),
                         lambda d, b, i: (b, jnp.minimum((chunk(d, i) + 1) * hb, S // LRU_HALO - 1), 0)),
            pl.BlockSpec((4, C), lambda d, b, i: (0, 0)),
            pl.BlockSpec((1, C), lambda d, b, i: (0, 0)),
            mat(), vec(), mat(), vec(), vec(),
        ],
        out_specs=pl.BlockSpec((None, None, L, C), lambda d, b, i: (d, b, chunk(d, i), 0)),
        out_shape=jax.ShapeDtypeStruct((2, B, S, C), F32),
        scratch_shapes=[pltpu.VMEM((L + 2 * LRU_HALO, C), F32), pltpu.VMEM((L, C), F32),
                        pltpu.VMEM((L, C), F32), pltpu.VMEM((1, C), F32)],
        compiler_params=_params("arbitrary", "arbitrary", "arbitrary"),
        name="rg_lru",
    )(rest, rest, rest, conv_w, conv_b.reshape(1, C), wa, ba.reshape(2, 1, C), wx, bx.reshape(2, 1, C),
      lam.reshape(2, 1, C))


def _merge_kernel(ya_ref, hf_ref, hb_ref, gr_ref, ga_ref, gb_ref, wa_ref, wr_ref, o_ref):
    yr = (hf_ref[...] + hb_ref[...]) * _gelu(gr_ref[...])
    pa = jnp.dot(ya_ref[...], wa_ref[...], preferred_element_type=F32)
    pr = jnp.dot(yr.astype(BF16), wr_ref[...], preferred_element_type=F32)
    o_ref[...] = (_sigmoid(ga_ref[...]) * pa + _sigmoid(gb_ref[...]) * pr).astype(o_ref.dtype)


def _merge(ya, hs, rest, wba, wbr, tm):
    T, C = ya.shape
    D = wba.shape[1]
    tm = min(tm, T)
    assert rest.shape[1] == 2 * C + 2 * D and D % C == 0
    return pl.pallas_call(
        _merge_kernel,
        grid=(T // tm,),
        in_specs=[
            pl.BlockSpec((tm, C), lambda i: (i, 0)),
            pl.BlockSpec((None, tm, C), lambda i: (0, i, 0)),
            pl.BlockSpec((None, tm, C), lambda i: (1, i, 0)),
            pl.BlockSpec((tm, C), lambda i: (i, 1)),
            pl.BlockSpec((tm, D), lambda i: (i, (2 * C) // D)),
            pl.BlockSpec((tm, D), lambda i: (i, (2 * C) // D + 1)),
            pl.BlockSpec((C, D), lambda i: (0, 0)),
            pl.BlockSpec((C, D), lambda i: (0, 0)),
        ],
        out_specs=pl.BlockSpec((tm, D), lambda i: (i, 0)),
        out_shape=jax.ShapeDtypeStruct((T, D), BF16),
        compiler_params=_params("parallel"),
        name="branch_merge",
    )(ya, hs, hs, rest, rest, rest, wba, wbr)


def _topk_rows(s, iota, n):
    vals, idxs = [], []
    for _ in range(PEER_TOPK):
        m = jnp.max(s, axis=0, keepdims=True)
        idx = jnp.min(jnp.where(s == m, iota, n), axis=0, keepdims=True)
        vals.append(m)
        idxs.append(idx)
        s = jnp.where(iota == idx, -jnp.inf, s)
    return jnp.concatenate(vals, axis=0), jnp.concatenate(idxs, axis=0)


def _route_kernel(h_ref, gn_ref, wq_ref, sk_ref, xn_ref, ids_ref, g_ref, q_s, ids_s, g_s, *, heads):
    tm = h_ref.shape[0]
    xn = _rms(h_ref[...], gn_ref[...])
    xn_ref[...] = xn
    q = jnp.dot(xn.astype(BF16), wq_ref[...], preferred_element_type=F32).astype(BF16)
    for hp in range(2 * heads):
        q_s[hp] = q[:, hp * PEER_NKEYS:(hp + 1) * PEER_NKEYS]
    K = PEER_TOPK
    iota_keys = lax.broadcasted_iota(I32, (PEER_NKEYS, tm), 0)
    iota_k = lax.broadcasted_iota(I32, (K, tm), 0)
    ka = 4
    kb = K // (ka + 1)
    cand_idx = jnp.concatenate([k1 * K + iota_k for k1 in range(ka)] + [iota_k * K + k2 for k2 in range(kb)], axis=0)

    def head_body(h, carry):
        sv, si = [], []
        for p in range(2):
            s = lax.dot_general(sk_ref[2 * h + p], q_s[2 * h + p], (((1,), (1,)), ((), ())),
                                preferred_element_type=F32)
            v, ix = _topk_rows(s, iota_keys, PEER_NKEYS)
            sv.append(v)
            si.append(ix)
        cand = jnp.concatenate(
            [sv[0][k1:k1 + 1, :] + sv[1] for k1 in range(ka)]
            + [jnp.where(iota_k >= ka, sv[0] + sv[1][k2:k2 + 1, :], -jnp.inf) for k2 in range(kb)], axis=0)
        cv, ci = _topk_rows(cand, cand_idx, K * K)
        ids = []
        for k in range(K):
            c = ci[k:k + 1, :]
            i1 = jnp.sum(jnp.where(iota_k == (c >> PEER_TOPK_LOG2), si[0], 0), axis=0, keepdims=True)
            i2 = jnp.sum(jnp.where(iota_k == (c & (K - 1)), si[1], 0), axis=0, keepdims=True)
            ids.append(i1 * PEER_NKEYS + i2)
        e = jnp.exp(cv - cv[0:1, :])
        row = pl.multiple_of(h * K, K)
        ids_s[pl.ds(row, K), :] = jnp.concatenate(ids, axis=0).astype(F32)
        g_s[pl.ds(row, K), :] = e / jnp.sum(e, axis=0, keepdims=True)
        return carry

    lax.fori_loop(0, heads, head_body, 0)
    ids_ref[...] = ids_s[...].T.astype(I32)
    g_ref[...] = g_s[...].T


def _route(h, gn, wq, sk, tm):
    T, D = h.shape
    NQ = wq.shape[1]
    heads = NQ // (2 * PEER_NKEYS)
    nsel = heads * PEER_TOPK
    tm = min(tm, T)
    return pl.pallas_call(
        functools.partial(_route_kernel, heads=heads),
        grid=(T // tm,),
        in_specs=[
            pl.BlockSpec((tm, D), lambda i: (i, 0)),
            pl.BlockSpec((1, D), lambda i: (0, 0)),
            pl.BlockSpec((D, NQ), lambda i: (0, 0)),
            pl.BlockSpec(sk.shape, lambda i: (0, 0, 0)),
        ],
        out_specs=[
            pl.BlockSpec((tm, D), lambda i: (i, 0)),
            pl.BlockSpec((tm, nsel), lambda i: (i, 0)),
            pl.BlockSpec((tm, nsel), lambda i: (i, 0)),
        ],
        out_shape=[
            jax.ShapeDtypeStruct((T, D), F32),
            jax.ShapeDtypeStruct((T, nsel), I32),
            jax.ShapeDtypeStruct((T, nsel), F32),
        ],
        scratch_shapes=[pltpu.VMEM((2 * heads, tm, PEER_NKEYS), BF16), pltpu.VMEM((nsel, tm), F32),
                        pltpu.VMEM((nsel, tm), F32)],
        compiler_params=_params("parallel"),
        name="peer_route",
    )(h, gn.reshape(1, D), wq, sk)


def _pack_bf16_pairs(lo, hi):
    lo16 = lax.bitcast_convert_type(lo.astype(BF16), jnp.uint16).astype(U32)
    hi16 = lax.bitcast_convert_type(hi.astype(BF16), jnp.uint16).astype(U32)
    return lo16 | (hi16 << 16)


def _unpack_lo(words):
    return lax.bitcast_convert_type(words << 16, F32)


def _unpack_hi(words):
    return lax.bitcast_convert_type(words & np.uint32(0xFFFF0000), F32)


def _gather_kernel(ids_hbm, x_ref, g_ref, h_ref, tab_hbm, o_ref, ids_s, sem, isem, *bufs, tb, nbuf, nsel):
    i = pl.program_id(0)
    D = x_ref.shape[1]
    nlt = D // LANES
    nsteps = pl.num_programs(0)
    win = (tb + nbuf) * nsel
    half = lax.rem(i, 2)

    def ids_copy(step, h):
        return pltpu.make_async_copy(ids_hbm.at[pl.ds(pl.multiple_of(step * (tb * nsel), 1024), win)],
                                     ids_s.at[pl.ds(pl.multiple_of(h * win, 1024), win)], isem.at[h])

    @pl.when(i == 0)
    def _():
        ids_copy(0, 0).start()

    ids_copy(i, half).wait()

    @pl.when(i + 1 < nsteps)
    def _():
        ids_copy(i + 1, 1 - half).start()

    ids_base = half * win

    def issue(t, slot, n0, n1):
        base = ids_base + t * nsel
        for n in range(n0, n1):
            e = ids_s[base + n]
            pltpu.make_async_copy(tab_hbm.at[e], bufs[slot].at[pl.ds(n, 1), :], sem.at[slot]).start(priority=n % 2)

    def wait(slot):
        pltpu.make_async_copy(bufs[slot], bufs[slot], sem.at[slot]).wait()

    @pl.when(i == 0)
    def _():
        for s in range(nbuf - 1):
            issue(s, s, 0, nsel)

    eye = lax.broadcasted_iota(I32, (nsel, nsel), 0) == lax.broadcasted_iota(I32, (nsel, nsel), 1)
    per_chunk = nsel // (2 * nlt)
    assert per_chunk * 2 * nlt == nsel

    def group(gi, carry):
        for b in range(nbuf):
            t = gi * nbuf + b
            nt = t + (nbuf - 1)
            nslot = (b + nbuf - 1) % nbuf
            wait(b)
            xb = x_ref[pl.ds(t, 1), :]
            acc = None
            for j in range(nlt):
                cs = slice(j * LANES, (j + 1) * LANES)
                term = _unpack_lo(bufs[b][:, cs]) * xb[:, cs]
                acc = term if acc is None else acc + term
                issue(nt, nslot, j * per_chunk, (j + 1) * per_chunk)
            score = jnp.sum(acc, axis=1, keepdims=True)
            gcol = jnp.sum(jnp.where(eye, g_ref[pl.ds(t, 1), :], 0.0), axis=1, keepdims=True)
            w = gcol * _gelu(score)
            ys = []
            for j in range(nlt):
                ys.append(jnp.sum(_unpack_hi(bufs[b][:, j * LANES:(j + 1) * LANES]) * w, axis=0, keepdims=True))
                issue(nt, nslot, (nlt + j) * per_chunk, (nlt + j + 1) * per_chunk)
            o_ref[pl.ds(t, 1), :] = h_ref[pl.ds(t, 1), :] + jnp.concatenate(ys, axis=1)
        return carry

    lax.fori_loop(0, tb // nbuf, group, 0)

    @pl.when(i == nsteps - 1)
    def _():
        for s in range(nbuf - 1):
            wait(s)


def _gather(ids, xn, g, h, tab, tb, nbuf):
    T, D = xn.shape
    nsel = g.shape[1]
    tb = min(tb, T)
    assert tb % nbuf == 0 and T % tb == 0
    ids = jnp.concatenate([ids, jnp.zeros((nbuf * nsel,), I32)])
    return pl.pallas_call(
        functools.partial(_gather_kernel, tb=tb, nbuf=nbuf, nsel=nsel),
        grid=(T // tb,),
        in_specs=[
            pl.BlockSpec(memory_space=pl.ANY),
            pl.BlockSpec((tb, D), lambda i: (i, 0)),
            pl.BlockSpec((tb, nsel), lambda i: (i, 0)),
            pl.BlockSpec((tb, D), lambda i: (i, 0)),
            pl.BlockSpec(memory_space=pl.ANY),
        ],
        out_specs=pl.BlockSpec((tb, D), lambda i: (i, 0)),
        out_shape=jax.ShapeDtypeStruct((T, D), F32),
        scratch_shapes=[pltpu.SMEM((2 * (tb + nbuf) * nsel,), I32), pltpu.SemaphoreType.DMA((nbuf,)),
                        pltpu.SemaphoreType.DMA((2,))] + [pltpu.VMEM((nsel, D), U32) for _ in range(nbuf)],
        compiler_params=_params("arbitrary"),
        name="peer_gather",
    )(ids, xn, g, h, tab.reshape(tab.shape[0], 1, D))


def _ple_kernel(h_ref, p_ref, gn_ref, wg_ref, wp_ref, gf_ref, o_ref):
    h = h_ref[...]
    hn = _rms(h, gn_ref[...]).astype(BF16)
    gate = _sigmoid(jnp.dot(hn, wg_ref[...], preferred_element_type=F32))
    pp = jnp.dot(p_ref[...].astype(BF16), wp_ref[...], preferred_element_type=F32)
    o_ref[...] = _rms(h + gate * pp, gf_ref[...])


def _ple(h, p, gn, wg, wp, gf, tm):
    T, D = h.shape
    P = p.shape[1]
    tm = min(tm, T)
    return pl.pallas_call(
        _ple_kernel,
        grid=(T // tm,),
        in_specs=[
            pl.BlockSpec((tm, D), lambda i: (i, 0)),
            pl.BlockSpec((tm, P), lambda i: (i, 0)),
            pl.BlockSpec((1, D), lambda i: (0, 0)),
            pl.BlockSpec((D, D), lambda i: (0, 0)),
            pl.BlockSpec((P, D), lambda i: (0, 0)),
            pl.BlockSpec((1, D), lambda i: (0, 0)),
        ],
        out_specs=pl.BlockSpec((tm, D), lambda i: (i, 0)),
        out_shape=jax.ShapeDtypeStruct((T, D), F32),
        compiler_params=_params("parallel"),
        name="ple_final",
    )(h, p, gn.reshape(1, D), wg, wp, gf.reshape(1, D))


def _trunk(x, p, w):
    B, S, D = x.shape
    T = B * S
    x2 = x.reshape(T, D)
    qkv = _norm_matmul(x2, w["norm_mix"], w["w_in_qkv"], BF16, *IN_PROJ_TILE)
    rest = _norm_matmul(x2, w["norm_mix"], w["w_in_rest"], F32, *IN_PROJ_TILE)
    ya = _na(qkv.reshape(B, S, -1), w["na_bias"])
    hs = _lru(rest.reshape(B, S, -1), w["conv_w"], w["conv_b"], w["lru_wa"], w["lru_ba"], w["lru_wx"],
              w["lru_bx"], w["lru_lambda"], LRU_CHUNK)
    C = hs.shape[-1]
    merged = _merge(ya.reshape(T, -1), hs.reshape(2, T, C), rest, w["w_branch_a"], w["w_branch_r"], ROW_TILE)
    h1 = _matmul_res(merged, w["w_out"], x2, *OUT_PROJ_TILE)
    xn, ids, g = _route(h1, w["norm_ffn"], w["peer_wq"], w["peer_subkeys"], ROW_TILE)
    h2 = _gather(ids.reshape(-1), xn, g, h1, w["peer_tab"], GATHER_TOKENS_PER_STEP, GATHER_SLOTS)
    out = _ple(h2, p.reshape(T, -1), w["norm_ple"], w["ple_gate_w"], w["ple_proj_w"], w["final_norm"], ROW_TILE)
    return out.reshape(B, S, D)


def kernel(x_prompt, x_sample, p_prompt, p_sample, norm_mix, w_in, na_rpb, conv_w, conv_b, lru_wa, lru_ba, lru_wx, lru_bx, lru_lambda, w_branch_a, w_branch_r, w_out, norm_ffn, peer_wq, peer_subkeys, peer_u, peer_v, norm_ple, ple_gate_w, ple_proj_w, final_norm):
    assert w_in.shape[0] == 1, "single layer"
    na_w = w_branch_a.shape[1]
    sk = peer_subkeys[0]
    w = dict(
        norm_mix=norm_mix[0],
        w_in_qkv=w_in[0][:, :3 * na_w].astype(BF16),
        w_in_rest=w_in[0][:, 3 * na_w:].astype(BF16),
        na_bias=_na_bias_table(na_rpb[0]),
        conv_w=conv_w[0], conv_b=conv_b[0],
        lru_wa=lru_wa[0].astype(BF16), lru_ba=lru_ba[0], lru_wx=lru_wx[0].astype(BF16), lru_bx=lru_bx[0],
        lru_lambda=lru_lambda[0],
        w_branch_a=w_branch_a[0].astype(BF16), w_branch_r=w_branch_r[0].astype(BF16),
        w_out=w_out[0].astype(BF16),
        norm_ffn=norm_ffn[0], peer_wq=peer_wq[0].astype(BF16),
        peer_subkeys=sk.reshape(sk.shape[0] * 2, sk.shape[2], sk.shape[3]).astype(BF16),
        peer_tab=_pack_bf16_pairs(peer_u[0], peer_v[0]),
        norm_ple=norm_ple[0], ple_gate_w=ple_gate_w[0].astype(BF16), ple_proj_w=ple_proj_w[0].astype(BF16),
        final_norm=final_norm,
    )
    return (_trunk(x_prompt, p_prompt[0], w), _trunk(x_sample, p_sample[0], w))
```

```python
import functools
import math

import numpy as np
import jax
import jax.numpy as jnp
from jax import lax
from jax.experimental import pallas as pl
from jax.experimental.pallas import tpu as pltpu

F32 = jnp.float32
BF16 = jnp.bfloat16
I32 = jnp.int32
U32 = jnp.uint32

LANES = 128
GRID_W = 64
NA_HEAD_DIM = 128
NA_KH = 8
NA_KW = 16
NA_ROWS_PER_STEP = 8
NEG_INF = -1e30
LRU_BLOCK = 128
LRU_C = 8.0
LRU_HALO = 8
PEER_NKEYS = 128
PEER_TOPK = 16
PEER_TOPK_LOG2 = 4
PEER_PAIR_SPLIT = 4
assert 1 << PEER_TOPK_LOG2 == PEER_TOPK
RMS_EPS = 1e-6

VMEM_LIMIT = 50 * 1024 * 1024
IN_PROJ_TILE = (1024, 1024)
OUT_PROJ_TILE = (512, 1024)
ROW_TILE = 256
LRU_CHUNK = 512
GATHER_TOKENS_PER_STEP = 128
GATHER_SLOTS = 8


def _rms(x, g):
    return x * lax.rsqrt(jnp.mean(x * x, axis=-1, keepdims=True) + RMS_EPS) * g


def _gelu(x):
    return 0.5 * x * (1.0 + lax.erf(x * np.float32(math.sqrt(0.5))))


def _sigmoid(x):
    return jax.nn.sigmoid(x)


def _params(*sem):
    return pltpu.CompilerParams(dimension_semantics=sem, vmem_limit_bytes=VMEM_LIMIT)


def _norm_matmul_kernel(x_ref, g_ref, w_ref, o_ref, xn_ref):
    @pl.when(pl.program_id(1) == 0)
    def _():
        xn_ref[...] = _rms(x_ref[...], g_ref[...]).astype(BF16)

    o_ref[...] = jnp.dot(xn_ref[...], w_ref[...], preferred_element_type=F32).astype(o_ref.dtype)


def _norm_matmul(x, g, w, out_dtype, tm, tn):
    T, D = x.shape
    N = w.shape[1]
    tm, tn = min(tm, T), min(tn, N)
    return pl.pallas_call(
        _norm_matmul_kernel,
        grid=(T // tm, N // tn),
        in_specs=[
            pl.BlockSpec((tm, D), lambda i, j: (i, 0)),
            pl.BlockSpec((1, D), lambda i, j: (0, 0)),
            pl.BlockSpec((D, tn), lambda i, j: (0, j)),
        ],
        out_specs=pl.BlockSpec((tm, tn), lambda i, j: (i, j)),
        out_shape=jax.ShapeDtypeStruct((T, N), out_dtype),
        scratch_shapes=[pltpu.VMEM((tm, D), BF16)],
        compiler_params=_params("parallel", "arbitrary"),
        name="norm_matmul",
    )(x, g.reshape(1, D), w)


def _matmul_res_kernel(a_ref, w_ref, r_ref, o_ref):
    o_ref[...] = r_ref[...] + jnp.dot(a_ref[...], w_ref[...], preferred_element_type=F32)


def _matmul_res(a, w, res, tm, tn):
    T, K = a.shape
    N = w.shape[1]
    tm, tn = min(tm, T), min(tn, N)
    return pl.pallas_call(
        _matmul_res_kernel,
        grid=(T // tm, N // tn),
        in_specs=[
            pl.BlockSpec((tm, K), lambda i, j: (i, 0)),
            pl.BlockSpec((K, tn), lambda i, j: (0, j)),
            pl.BlockSpec((tm, tn), lambda i, j: (i, j)),
        ],
        out_specs=pl.BlockSpec((tm, tn), lambda i, j: (i, j)),
        out_shape=jax.ShapeDtypeStruct((T, N), F32),
        compiler_params=_params("parallel", "arbitrary"),
        name="matmul_res",
    )(a, w, res)


def _na_bias_table(rpb):
    qc = np.arange(GRID_W)
    kc = np.arange(GRID_W)
    col_start = np.clip(qc - NA_KW // 2, 0, GRID_W - NA_KW)
    ok = (kc[None, :] >= col_start[:, None]) & (kc[None, :] < col_start[:, None] + NA_KW)
    dc = kc[None, :] - qc[:, None] + (NA_KW - 1)
    onehot = ((dc[None] == np.arange(2 * NA_KW - 1)[:, None, None]) & ok[None]).astype(np.float32)
    t = jnp.einsum("hrd,dqk->hrqk", rpb.astype(F32), onehot, precision=lax.Precision.HIGHEST)
    t = jnp.where(ok[None, None], t, NEG_INF)
    per_off = [jnp.concatenate([t[:, slot - off + NA_KH - 1] for slot in range(NA_KH)], axis=-1)
               for off in range(NA_KH)]
    return jnp.stack(per_off, axis=1)


def _na_kernel(q_ref, kp_ref, kc_ref, kn_ref, vp_ref, vc_ref, vn_ref, b_ref, o_ref, kbuf, vbuf,
               *, rows, heads):
    i = pl.program_id(1)
    tb = NA_ROWS_PER_STEP * GRID_W
    nkeys = NA_KH * GRID_W
    halo = (NA_KH // 2) * GRID_W
    kbuf[0:halo] = kp_ref[...]
    kbuf[halo:halo + tb] = kc_ref[...]
    kbuf[halo + tb:2 * halo + tb] = kn_ref[...]
    vbuf[0:halo] = vp_ref[...]
    vbuf[halo:halo + tb] = vc_ref[...]
    vbuf[halo + tb:2 * halo + tb] = vn_ref[...]
    scale = np.float32(NA_HEAD_DIM ** -0.5)

    def row_body(rr, carry):
        r = i * NA_ROWS_PER_STEP + rr
        rs = jnp.clip(r - NA_KH // 2, 0, rows - NA_KH)
        woff = r - rs
        koff = pl.multiple_of((rs - (i * NA_ROWS_PER_STEP - NA_KH // 2)) * GRID_W, GRID_W)
        qoff = pl.multiple_of(rr * GRID_W, GRID_W)
        cols = [slice(h * NA_HEAD_DIM, (h + 1) * NA_HEAD_DIM) for h in range(heads)]
        scores = [lax.dot_general(q_ref[pl.ds(qoff, GRID_W), cs], kbuf[pl.ds(koff, nkeys), cs],
                                  (((1,), (1,)), ((), ())), preferred_element_type=F32) for cs in cols]
        probs, norms = [], []
        for h in range(heads):
            s = scores[h] * scale + b_ref[h, woff]
            p = jnp.exp(s - jnp.max(s, axis=-1, keepdims=True))
            norms.append(jnp.sum(p, axis=-1, keepdims=True))
            probs.append(p.astype(BF16))
        outs = [jnp.dot(probs[h], vbuf[pl.ds(koff, nkeys), cols[h]], preferred_element_type=F32) / norms[h]
                for h in range(heads)]
        o_ref[pl.ds(qoff, GRID_W), :] = jnp.concatenate(outs, axis=1).astype(o_ref.dtype)
        return carry

    lax.fori_loop(0, NA_ROWS_PER_STEP, row_body, 0)


def _na(qkv, bias):
    B, S, W3 = qkv.shape
    W = W3 // 3
    heads = W // NA_HEAD_DIM
    rows = S // GRID_W
    assert rows % NA_ROWS_PER_STEP == 0 and rows >= NA_KH
    nblk = rows // NA_ROWS_PER_STEP
    tb = NA_ROWS_PER_STEP * GRID_W

    halo = (NA_KH // 2) * GRID_W
    hpb = tb // halo
    assert hpb * halo == tb

    def spec(col):
        return pl.BlockSpec((None, tb, W), lambda b, i: (b, i, col))

    def above(col):
        return pl.BlockSpec((None, halo, W), lambda b, i: (b, jnp.maximum(i * hpb - 1, 0), col))

    def below(col):
        return pl.BlockSpec((None, halo, W), lambda b, i: (b, jnp.minimum((i + 1) * hpb, nblk * hpb - 1), col))

    return pl.pallas_call(
        functools.partial(_na_kernel, rows=rows, heads=heads),
        grid=(B, nblk),
        in_specs=[spec(0), above(1), spec(1), below(1), above(2), spec(2), below(2),
                  pl.BlockSpec(bias.shape, lambda b, i: (0, 0, 0, 0))],
        out_specs=pl.BlockSpec((None, tb, W), lambda b, i: (b, i, 0)),
        out_shape=jax.ShapeDtypeStruct((B, S, W), BF16),
        scratch_shapes=[pltpu.VMEM((tb + 2 * halo, W), BF16), pltpu.VMEM((tb + 2 * halo, W), BF16)],
        compiler_params=_params("parallel", "arbitrary"),
        name="na_attention",
    )(qkv, qkv, qkv, qkv, qkv, qkv, qkv, bias)


def _lru_kernel(xp_ref, x_ref, xn_ref, cw_ref, cb_ref, wa_ref, ba_ref, wx_ref, bx_ref, lam_ref,
                o_ref, xe, a_s, u_s, carry, *, L, nch, nblk):
    d = pl.program_id(0)
    i = pl.program_id(2)
    c = jnp.where(d == 0, i, nch - 1 - i)

    @pl.when(i == 0)
    def _():
        carry[...] = jnp.zeros_like(carry)

    xe[0:LRU_HALO] = jnp.where(c > 0, xp_ref[...], 0.0)
    xe[LRU_HALO:LRU_HALO + L] = x_ref[...]
    xe[LRU_HALO + L:2 * LRU_HALO + L] = jnp.where(c < nch - 1, xn_ref[...], 0.0)
    cw = cw_ref[...]
    xc = cb_ref[...] + cw[2:3] * xe[pl.ds(LRU_HALO, L), :]
    xc = xc + cw[0:1] * xe[pl.ds(LRU_HALO - 2, L), :]
    xc = xc + cw[1:2] * xe[pl.ds(LRU_HALO - 1, L), :]
    xc = xc + cw[3:4] * xe[pl.ds(LRU_HALO + 1, L), :]
    xcb = xc.astype(BF16)

    def blockdiag(w_ref):
        outs = []
        for n in range(nblk):
            cs = slice(n * LRU_BLOCK, (n + 1) * LRU_BLOCK)
            outs.append(jnp.dot(xcb[:, cs], w_ref[n], preferred_element_type=F32))
        return jnp.concatenate(outs, axis=1)

    rec = _sigmoid(blockdiag(wa_ref) + ba_ref[...])
    inp = _sigmoid(blockdiag(wx_ref) + bx_ref[...])
    lam = lam_ref[...]
    softplus_neg_lam = jnp.maximum(-lam, 0.0) + jnp.log1p(jnp.exp(-jnp.abs(lam)))
    log_a = -LRU_C * rec * softplus_neg_lam
    a_s[...] = jnp.exp(log_a)
    th = jnp.tanh(log_a)
    u_s[...] = jnp.sqrt(-2.0 * th / (1.0 - th)) * (inp * xc)

    def step(t, h):
        tt = jnp.where(d == 0, t, L - 1 - t)
        h = a_s[pl.ds(tt, 1), :] * h + u_s[pl.ds(tt, 1), :]
        o_ref[pl.ds(tt, 1), :] = h
        return h

    carry[...] = lax.fori_loop(0, L, step, carry[...], unroll=8)


def _lru(rest, conv_w, conv_b, wa, ba, wx, bx, lam, L):
    B, S, _ = rest.shape
    C = conv_w.shape[1]
    nblk = C // LRU_BLOCK
    L = min(L, S)
    nch = S // L
    hb = L // LRU_HALO

    def chunk(d, i):
        return jnp.where(d == 0, i, nch - 1 - i)

    vec = lambda: pl.BlockSpec((None, 1, C), lambda d, b, i: (d, 0, 0))
    mat = lambda: pl.BlockSpec((None, nblk, LRU_BLOCK, LRU_BLOCK), lambda d, b, i: (d, 0, 0, 0))
    return pl.pallas_call(
        functools.partial(_lru_kernel, L=L, nch=nch, nblk=nblk),
        grid=(2, B, nch),
        in_specs=[
            pl.BlockSpec((None, LRU_HALO, C), lambda d, b, i: (b, jnp.maximum(chunk(d, i) * hb - 1, 0), 0)),
            pl.BlockSpec((None, L, C), lambda d, b, i: (b, chunk(d, i), 0)),
            pl.BlockSpec((None, LRU_HALO, C),
                         lambda d, b, i: (b, jnp.minimum((chunk(d, i) + 1) * hb, S // LRU_HALO - 1), 0)),
            pl.BlockSpec((4, C), lambda d, b, i: (0, 0)),
            pl.BlockSpec((1, C), lambda d, b, i: (0, 0)),
            mat(), vec(), mat(), vec(), vec(),
        ],
        out_specs=pl.BlockSpec((None, None, L, C), lambda d, b, i: (d, b, chunk(d, i), 0)),
        out_shape=jax.ShapeDtypeStruct((2, B, S, C), F32),
        scratch_shapes=[pltpu.VMEM((L + 2 * LRU_HALO, C), F32), pltpu.VMEM((L, C), F32),
                        pltpu.VMEM((L, C), F32), pltpu.VMEM((1, C), F32)],
        compiler_params=_params("arbitrary", "arbitrary", "arbitrary"),
        name="rg_lru",
    )(rest, rest, rest, conv_w, conv_b.reshape(1, C), wa, ba.reshape(2, 1, C), wx, bx.reshape(2, 1, C),
      lam.reshape(2, 1, C))


def _merge_kernel(ya_ref, hf_ref, hb_ref, gr_ref, ga_ref, gb_ref, wa_ref, wr_ref, o_ref):
    yr = (hf_ref[...] + hb_ref[...]) * _gelu(gr_ref[...])
    pa = jnp.dot(ya_ref[...], wa_ref[...], preferred_element_type=F32)
    pr = jnp.dot(yr.astype(BF16), wr_ref[...], preferred_element_type=F32)
    o_ref[...] = (_sigmoid(ga_ref[...]) * pa + _sigmoid(gb_ref[...]) * pr).astype(o_ref.dtype)


def _merge(ya, hs, rest, wba, wbr, tm):
    T, C = ya.shape
    D = wba.shape[1]
    tm = min(tm, T)
    assert rest.shape[1] == 2 * C + 2 * D and D % C == 0
    return pl.pallas_call(
        _merge_kernel,
        grid=(T // tm,),
        in_specs=[
            pl.BlockSpec((tm, C), lambda i: (i, 0)),
            pl.BlockSpec((None, tm, C), lambda i: (0, i, 0)),
            pl.BlockSpec((None, tm, C), lambda i: (1, i, 0)),
            pl.BlockSpec((tm, C), lambda i: (i, 1)),
            pl.BlockSpec((tm, D), lambda i: (i, (2 * C) // D)),
            pl.BlockSpec((tm, D), lambda i: (i, (2 * C) // D + 1)),
            pl.BlockSpec((C, D), lambda i: (0, 0)),
            pl.BlockSpec((C, D), lambda i: (0, 0)),
        ],
        out_specs=pl.BlockSpec((tm, D), lambda i: (i, 0)),
        out_shape=jax.ShapeDtypeStruct((T, D), BF16),
        compiler_params=_params("parallel"),
        name="branch_merge",
    )(ya, hs, hs, rest, rest, rest, wba, wbr)


def _topk_rows(s, iota, n):
    vals, idxs = [], []
    for _ in range(PEER_TOPK):
        m = jnp.max(s, axis=0, keepdims=True)
        idx = jnp.min(jnp.where(s == m, iota, n), axis=0, keepdims=True)
        vals.append(m)
        idxs.append(idx)
        s = jnp.where(iota == idx, -jnp.inf, s)
    return jnp.concatenate(vals, axis=0), jnp.concatenate(idxs, axis=0)


def _route_prologue(h_ref, gn_ref, wq_ref, xn_ref, q_s, heads):
    xn = _rms(h_ref[...], gn_ref[...])
    xn_ref[...] = xn
    q = jnp.dot(xn.astype(BF16), wq_ref[...], preferred_element_type=F32).astype(BF16)
    for hp in range(2 * heads):
        q_s[hp] = q[:, hp * PEER_NKEYS:(hp + 1) * PEER_NKEYS]


def _route_head_fn(tm, sk_ref, q_s, ids_s, g_s):
    K = PEER_TOPK
    iota_keys = lax.broadcasted_iota(I32, (PEER_NKEYS, tm), 0)
    iota_k = lax.broadcasted_iota(I32, (K, tm), 0)
    ka = PEER_PAIR_SPLIT
    kb = K // (ka + 1)
    cand_idx = jnp.concatenate([k1 * K + iota_k for k1 in range(ka)] + [iota_k * K + k2 for k2 in range(kb)], axis=0)

    def head(h):
        sv, si = [], []
        for p in range(2):
            s = lax.dot_general(sk_ref[2 * h + p], q_s[2 * h + p], (((1,), (1,)), ((), ())),
                                preferred_element_type=F32)
            v, ix = _topk_rows(s, iota_keys, PEER_NKEYS)
            sv.append(v)
            si.append(ix)
        cand = jnp.concatenate(
            [sv[0][k1:k1 + 1, :] + sv[1] for k1 in range(ka)]
            + [jnp.where(iota_k >= ka, sv[0] + sv[1][k2:k2 + 1, :], -jnp.inf) for k2 in range(kb)], axis=0)
        cv, ci = _topk_rows(cand, cand_idx, K * K)
        ids = []
        for k in range(K):
            c = ci[k:k + 1, :]
            i1 = jnp.sum(jnp.where(iota_k == (c >> PEER_TOPK_LOG2), si[0], 0), axis=0, keepdims=True)
            i2 = jnp.sum(jnp.where(iota_k == (c & (K - 1)), si[1], 0), axis=0, keepdims=True)
            ids.append(i1 * PEER_NKEYS + i2)
        e = jnp.exp(cv - cv[0:1, :])
        row = pl.multiple_of(h * K, K)
        ids_s[pl.ds(row, K), :] = jnp.concatenate(ids, axis=0).astype(F32)
        g_s[pl.ds(row, K), :] = e / jnp.sum(e, axis=0, keepdims=True)

    return head


def _route_epilogue(ids_ref, g_ref, ids_s, g_s):
    ids_ref[...] = ids_s[...].T.astype(I32)
    g_ref[...] = g_s[...].T


def _route_kernel(h_ref, gn_ref, wq_ref, sk_ref, xn_ref, ids_ref, g_ref, q_s, ids_s, g_s, *, heads):
    _route_prologue(h_ref, gn_ref, wq_ref, xn_ref, q_s, heads)
    head = _route_head_fn(h_ref.shape[0], sk_ref, q_s, ids_s, g_s)

    def head_body(h, carry):
        head(h)
        return carry

    lax.fori_loop(0, heads, head_body, 0)
    _route_epilogue(ids_ref, g_ref, ids_s, g_s)


def _route(h, gn, wq, sk, tm):
    T, D = h.shape
    NQ = wq.shape[1]
    heads = NQ // (2 * PEER_NKEYS)
    nsel = heads * PEER_TOPK
    tm = min(tm, T)
    return pl.pallas_call(
        functools.partial(_route_kernel, heads=heads),
        grid=(T // tm,),
        in_specs=[
            pl.BlockSpec((tm, D), lambda i: (i, 0)),
            pl.BlockSpec((1, D), lambda i: (0, 0)),
            pl.BlockSpec((D, NQ), lambda i: (0, 0)),
            pl.BlockSpec(sk.shape, lambda i: (0, 0, 0)),
        ],
        out_specs=[
            pl.BlockSpec((tm, D), lambda i: (i, 0)),
            pl.BlockSpec((tm, nsel), lambda i: (i, 0)),
            pl.BlockSpec((tm, nsel), lambda i: (i, 0)),
        ],
        out_shape=[
            jax.ShapeDtypeStruct((T, D), F32),
            jax.ShapeDtypeStruct((T, nsel), I32),
            jax.ShapeDtypeStruct((T, nsel), F32),
        ],
        scratch_shapes=[pltpu.VMEM((2 * heads, tm, PEER_NKEYS), BF16), pltpu.VMEM((nsel, tm), F32),
                        pltpu.VMEM((nsel, tm), F32)],
        compiler_params=_params("parallel"),
        name="peer_route",
    )(h, gn.reshape(1, D), wq, sk)


def _pack_bf16_pairs(lo, hi):
    lo16 = lax.bitcast_convert_type(lo.astype(BF16), jnp.uint16).astype(U32)
    hi16 = lax.bitcast_convert_type(hi.astype(BF16), jnp.uint16).astype(U32)
    return lo16 | (hi16 << 16)


def _unpack_lo(words):
    return lax.bitcast_convert_type(words << 16, F32)


def _unpack_hi(words):
    return lax.bitcast_convert_type(words & np.uint32(0xFFFF0000), F32)


def _gather_impl(ids_hbm, x_ref, g_ref, h_ref, tab_hbm, o_ref, ids_s, sem, isem, bufs, tb, nbuf, nsel, side=None):
    i = pl.program_id(0)
    D = x_ref.shape[1]
    nlt = D // LANES
    nsteps = pl.num_programs(0)
    win = (tb + nbuf) * nsel
    half = lax.rem(i, 2)

    def ids_copy(step, h):
        return pltpu.make_async_copy(ids_hbm.at[pl.ds(pl.multiple_of(step * (tb * nsel), 1024), win)],
                                     ids_s.at[pl.ds(pl.multiple_of(h * win, 1024), win)], isem.at[h])

    @pl.when(i == 0)
    def _():
        ids_copy(0, 0).start()

    ids_copy(i, half).wait()

    @pl.when(i + 1 < nsteps)
    def _():
        ids_copy(i + 1, 1 - half).start()

    ids_base = half * win

    def issue(t, slot, n0, n1):
        base = ids_base + t * nsel
        for n in range(n0, n1):
            e = ids_s[base + n]
            pltpu.make_async_copy(tab_hbm.at[e], bufs[slot].at[pl.ds(n, 1), :], sem.at[slot]).start(priority=n % 2)

    def wait(slot):
        pltpu.make_async_copy(bufs[slot], bufs[slot], sem.at[slot]).wait()

    @pl.when(i == 0)
    def _():
        for s in range(nbuf - 1):
            issue(s, s, 0, nsel)

    eye = lax.broadcasted_iota(I32, (nsel, nsel), 0) == lax.broadcasted_iota(I32, (nsel, nsel), 1)
    per_chunk = nsel // (2 * nlt)
    assert per_chunk * 2 * nlt == nsel

    def token(t, b):
        nt = t + (nbuf - 1)
        nslot = (b + nbuf - 1) % nbuf
        wait(b)
        xb = x_ref[pl.ds(t, 1), :]
        acc = None
        for j in range(nlt):
            cs = slice(j * LANES, (j + 1) * LANES)
            term = _unpack_lo(bufs[b][:, cs]) * xb[:, cs]
            acc = term if acc is None else acc + term
            issue(nt, nslot, j * per_chunk, (j + 1) * per_chunk)
        score = jnp.sum(acc, axis=1, keepdims=True)
        gcol = jnp.sum(jnp.where(eye, g_ref[pl.ds(t, 1), :], 0.0), axis=1, keepdims=True)
        w = gcol * _gelu(score)
        ys = []
        for j in range(nlt):
            ys.append(jnp.sum(_unpack_hi(bufs[b][:, j * LANES:(j + 1) * LANES]) * w, axis=0, keepdims=True))
            issue(nt, nslot, (nlt + j) * per_chunk, (nlt + j + 1) * per_chunk)
        o_ref[pl.ds(t, 1), :] = h_ref[pl.ds(t, 1), :] + jnp.concatenate(ys, axis=1)

    if side is None:
        per_unit, n_units, unit = nbuf, tb // nbuf, None
    else:
        prologue, unit, epilogue, n_units = side
        per_unit = tb // n_units
        assert per_unit * n_units == tb and per_unit % nbuf == 0
        prologue()

    def group(gi, carry):
        for tt in range(per_unit):
            token(gi * per_unit + tt, tt % nbuf)
        if unit is not None:
            unit(gi)
        return carry

    lax.fori_loop(0, n_units, group, 0)
    if side is not None:
        epilogue()

    @pl.when(i == nsteps - 1)
    def _():
        for s in range(nbuf - 1):
            wait(s)


def _gather_kernel(ids_hbm, x_ref, g_ref, h_ref, tab_hbm, o_ref, ids_s, sem, isem, *bufs, tb, nbuf, nsel):
    _gather_impl(ids_hbm, x_ref, g_ref, h_ref, tab_hbm, o_ref, ids_s, sem, isem, bufs, tb, nbuf, nsel)


def _gather_route_kernel(ids_hbm, x_ref, g_ref, h_ref, tab_hbm, rh_ref, gn_ref, wq_ref, sk_ref,
                         o_ref, xn_ref, rids_ref, rg_ref, ids_s, sem, isem, q_s, rids_s, rg_s, *bufs,
                         tb, nbuf, nsel, heads):
    head = _route_head_fn(tb, sk_ref, q_s, rids_s, rg_s)
    side = (lambda: _route_prologue(rh_ref, gn_ref, wq_ref, xn_ref, q_s, heads), head,
            lambda: _route_epilogue(rids_ref, rg_ref, rids_s, rg_s), heads)
    _gather_impl(ids_hbm, x_ref, g_ref, h_ref, tab_hbm, o_ref, ids_s, sem, isem, bufs, tb, nbuf, nsel, side)


def _gather_specs(T, D, nsel, tb, nbuf):
    in_specs = [
        pl.BlockSpec(memory_space=pl.ANY),
        pl.BlockSpec((tb, D), lambda i: (i, 0)),
        pl.BlockSpec((tb, nsel), lambda i: (i, 0)),
        pl.BlockSpec((tb, D), lambda i: (i, 0)),
        pl.BlockSpec(memory_space=pl.ANY),
    ]
    scratch = [pltpu.SMEM((2 * (tb + nbuf) * nsel,), I32), pltpu.SemaphoreType.DMA((nbuf,)),
               pltpu.SemaphoreType.DMA((2,))]
    bufs = [pltpu.VMEM((nsel, D), U32) for _ in range(nbuf)]
    return in_specs, scratch, bufs


def _gather(ids, xn, g, h, tab, tb, nbuf):
    T, D = xn.shape
    nsel = g.shape[1]
    tb = min(tb, T)
    assert tb % nbuf == 0 and T % tb == 0
    ids = jnp.concatenate([ids, jnp.zeros((nbuf * nsel,), I32)])
    in_specs, scratch, bufs = _gather_specs(T, D, nsel, tb, nbuf)
    return pl.pallas_call(
        functools.partial(_gather_kernel, tb=tb, nbuf=nbuf, nsel=nsel),
        grid=(T // tb,),
        in_specs=in_specs,
        out_specs=pl.BlockSpec((tb, D), lambda i: (i, 0)),
        out_shape=jax.ShapeDtypeStruct((T, D), F32),
        scratch_shapes=scratch + bufs,
        compiler_params=_params("arbitrary"),
        name="peer_gather",
    )(ids, xn, g, h, tab.reshape(tab.shape[0], 1, D))


def _gather_route(ids, xn, g, h, tab, rh, gn, wq, sk, tb, nbuf):
    T, D = xn.shape
    nsel = g.shape[1]
    NQ = wq.shape[1]
    heads = NQ // (2 * PEER_NKEYS)
    assert rh.shape == (T, D) and heads * PEER_TOPK == nsel and tb % nbuf == 0 and T % tb == 0
    ids = jnp.concatenate([ids, jnp.zeros((nbuf * nsel,), I32)])
    in_specs, scratch, bufs = _gather_specs(T, D, nsel, tb, nbuf)
    row = lambda width: pl.BlockSpec((tb, width), lambda i: (i, 0))
    return pl.pallas_call(
        functools.partial(_gather_route_kernel, tb=tb, nbuf=nbuf, nsel=nsel, heads=heads),
        grid=(T // tb,),
        in_specs=in_specs + [row(D), pl.BlockSpec((1, D), lambda i: (0, 0)),
                             pl.BlockSpec((D, NQ), lambda i: (0, 0)), pl.BlockSpec(sk.shape, lambda i: (0, 0, 0))],
        out_specs=[row(D), row(D), row(nsel), row(nsel)],
        out_shape=[jax.ShapeDtypeStruct((T, D), F32), jax.ShapeDtypeStruct((T, D), F32),
                   jax.ShapeDtypeStruct((T, nsel), I32), jax.ShapeDtypeStruct((T, nsel), F32)],
        scratch_shapes=scratch + [pltpu.VMEM((2 * heads, tb, PEER_NKEYS), BF16), pltpu.VMEM((nsel, tb), F32),
                                  pltpu.VMEM((nsel, tb), F32)] + bufs,
        compiler_params=_params("arbitrary"),
        name="peer_gather_route",
    )(ids, xn, g, h, tab.reshape(tab.shape[0], 1, D), rh, gn.reshape(1, D), wq, sk)


def _ple_kernel(h_ref, p_ref, gn_ref, wg_ref, wp_ref, gf_ref, o_ref):
    h = h_ref[...]
    hn = _rms(h, gn_ref[...]).astype(BF16)
    gate = _sigmoid(jnp.dot(hn, wg_ref[...], preferred_element_type=F32))
    pp = jnp.dot(p_ref[...].astype(BF16), wp_ref[...], preferred_element_type=F32)
    o_ref[...] = _rms(h + gate * pp, gf_ref[...])


def _ple(h, p, gn, wg, wp, gf, tm):
    T, D = h.shape
    P = p.shape[1]
    tm = min(tm, T)
    return pl.pallas_call(
        _ple_kernel,
        grid=(T // tm,),
        in_specs=[
            pl.BlockSpec((tm, D), lambda i: (i, 0)),
            pl.BlockSpec((tm, P), lambda i: (i, 0)),
            pl.BlockSpec((1, D), lambda i: (0, 0)),
            pl.BlockSpec((D, D), lambda i: (0, 0)),
            pl.BlockSpec((P, D), lambda i: (0, 0)),
            pl.BlockSpec((1, D), lambda i: (0, 0)),
        ],
        out_specs=pl.BlockSpec((tm, D), lambda i: (i, 0)),
        out_shape=jax.ShapeDtypeStruct((T, D), F32),
        compiler_params=_params("parallel"),
        name="ple_final",
    )(h, p, gn.reshape(1, D), wg, wp, gf.reshape(1, D))


def _mix(x, w):
    B, S, D = x.shape
    T = B * S
    x2 = x.reshape(T, D)
    qkv = _norm_matmul(x2, w["norm_mix"], w["w_in_qkv"], BF16, *IN_PROJ_TILE)
    rest = _norm_matmul(x2, w["norm_mix"], w["w_in_rest"], F32, *IN_PROJ_TILE)
    ya = _na(qkv.reshape(B, S, -1), w["na_bias"])
    hs = _lru(rest.reshape(B, S, -1), w["conv_w"], w["conv_b"], w["lru_wa"], w["lru_ba"], w["lru_wx"],
              w["lru_bx"], w["lru_lambda"], LRU_CHUNK)
    C = hs.shape[-1]
    merged = _merge(ya.reshape(T, -1), hs.reshape(2, T, C), rest, w["w_branch_a"], w["w_branch_r"], ROW_TILE)
    return _matmul_res(merged, w["w_out"], x2, *OUT_PROJ_TILE)


def _peer(h_groups, w):
    route = lambda h, tm: _route(h, w["norm_ffn"], w["peer_wq"], w["peer_subkeys"], tm)
    gather = lambda ids, xn, g, h: _gather(ids.reshape(-1), xn, g, h, w["peer_tab"], GATHER_TOKENS_PER_STEP,
                                           GATHER_SLOTS)
    if len(h_groups) == 2 and h_groups[0].shape == h_groups[1].shape and \
            h_groups[0].shape[0] % GATHER_TOKENS_PER_STEP == 0:
        ha, hb = h_groups
        xa, ia, ga = route(ha, ROW_TILE)
        oa, xb, ib, gb = _gather_route(ia.reshape(-1), xa, ga, ha, w["peer_tab"], hb, w["norm_ffn"], w["peer_wq"],
                                       w["peer_subkeys"], GATHER_TOKENS_PER_STEP, GATHER_SLOTS)
        return [oa, gather(ib, xb, gb, hb)]
    outs = []
    for h in h_groups:
        xn, ids, g = route(h, ROW_TILE)
        outs.append(gather(ids, xn, g, h))
    return outs


def _finish(h, p, shape, w):
    out = _ple(h, p.reshape(h.shape[0], -1), w["norm_ple"], w["ple_gate_w"], w["ple_proj_w"], w["final_norm"], ROW_TILE)
    return out.reshape(shape)


def kernel(x_prompt, x_sample, p_prompt, p_sample, norm_mix, w_in, na_rpb, conv_w, conv_b, lru_wa, lru_ba, lru_wx, lru_bx, lru_lambda, w_branch_a, w_branch_r, w_out, norm_ffn, peer_wq, peer_subkeys, peer_u, peer_v, norm_ple, ple_gate_w, ple_proj_w, final_norm):
    assert w_in.shape[0] == 1, "single layer"
    na_w = w_branch_a.shape[1]
    sk = peer_subkeys[0]
    w = dict(
        norm_mix=norm_mix[0],
        w_in_qkv=w_in[0][:, :3 * na_w].astype(BF16),
        w_in_rest=w_in[0][:, 3 * na_w:].astype(BF16),
        na_bias=_na_bias_table(na_rpb[0]),
        conv_w=conv_w[0], conv_b=conv_b[0],
        lru_wa=lru_wa[0].astype(BF16), lru_ba=lru_ba[0], lru_wx=lru_wx[0].astype(BF16), lru_bx=lru_bx[0],
        lru_lambda=lru_lambda[0],
        w_branch_a=w_branch_a[0].astype(BF16), w_branch_r=w_branch_r[0].astype(BF16),
        w_out=w_out[0].astype(BF16),
        norm_ffn=norm_ffn[0], peer_wq=peer_wq[0].astype(BF16),
        peer_subkeys=sk.reshape(sk.shape[0] * 2, sk.shape[2], sk.shape[3]).astype(BF16),
        peer_tab=_pack_bf16_pairs(peer_u[0], peer_v[0]),
        norm_ple=norm_ple[0], ple_gate_w=ple_gate_w[0].astype(BF16), ple_proj_w=ple_proj_w[0].astype(BF16),
        final_norm=final_norm,
    )
    h = _peer([_mix(x_prompt, w), _mix(x_sample, w)], w)
    return (_finish(h[0], p_prompt[0], x_prompt.shape, w), _finish(h[1], p_sample[0], x_sample.shape, w))
```
